```python
import math
import jax, jax.numpy as jnp
from jax import lax
import numpy as np

D_MODEL = 1024
BATCH = 8
SEQ = 4096
DEPTH = 1

EXPAND = 2
D_MIX = EXPAND * D_MODEL
D_CONV = D_MIX // 2
D_LRU = D_MIX - D_CONV
N_CONV_HEADS = 8
N_LRU_HEADS = 16
LRU_HEAD_DIM = D_LRU // N_LRU_HEADS
SHORT_CONV_WIDTH = 3
LRU_CONV_WIDTH = 4
RG_LRU_C = 8.0
RMS_EPS = 1e-6
IN_COLS = 4 * D_CONV + 2 * D_LRU

kernel_name = "hymba_shortconv_rglru_hybrid"


def rms_norm(x, g):
    xf = x.astype(jnp.float32)
    xf = xf * lax.rsqrt(jnp.mean(xf * xf, axis=-1, keepdims=True) + RMS_EPS)
    return xf.astype(x.dtype) * g


def headwise_rms_norm(y, n_heads, g):
    b, s, d = y.shape
    yh = y.reshape(b, s, n_heads, d // n_heads).astype(jnp.float32)
    yh = yh * lax.rsqrt(jnp.mean(yh * yh, axis=-1, keepdims=True) + RMS_EPS)
    return yh.reshape(b, s, d).astype(y.dtype) * g


def causal_depthwise_conv(u, w):
    k_width = w.shape[0]
    s = u.shape[1]
    up = jnp.pad(u, ((0, 0), (k_width - 1, 0), (0, 0)))
    out = up[:, 0:s, :] * w[0]
    for k in range(1, k_width):
        out = out + up[:, k:k + s, :] * w[k]
    return out


def short_conv_mixer(b_gate, c_gate, x_in, conv_w):
    return b_gate * causal_depthwise_conv(c_gate * x_in, conv_w)


def _lru_combine(left, right):
    a1, b1 = left
    a2, b2 = right
    return a1 * a2, a2 * b1 + b2


def rg_lru_mixer(x_in, conv_w, conv_b, w_a, b_a, w_i, b_i, lam):
    bsz, s, d = x_in.shape
    u = causal_depthwise_conv(x_in, conv_w) + conv_b
    uh = u.reshape(bsz, s, N_LRU_HEADS, LRU_HEAD_DIM)
    r = jax.nn.sigmoid(jnp.einsum('bshd,hde->bshe', uh, w_a).reshape(bsz, s, d) + b_a)
    i = jax.nn.sigmoid(jnp.einsum('bshd,hde->bshe', uh, w_i).reshape(bsz, s, d) + b_i)
    log_a = RG_LRU_C * r.astype(jnp.float32) * jax.nn.log_sigmoid(lam.astype(jnp.float32))
    a = jnp.exp(log_a)
    mult = jnp.sqrt(-jnp.expm1(2.0 * log_a))
    drive = mult * (i * u).astype(jnp.float32)
    _, h = lax.associative_scan(_lru_combine, (a, drive), axis=1)
    return h.astype(x_in.dtype)


def setup_inputs(seed: int = 0) -> dict:
    key = jax.random.key(seed)
    ks = jax.random.split(key, 16)
    f32 = jnp.float32
    x = jax.random.normal(ks[0], (BATCH, SEQ, D_MODEL), f32)
    ln_g = 1.0 + 0.02 * jax.random.normal(ks[1], (D_MODEL,), f32)
    w_in = jax.random.normal(ks[2], (D_MODEL, IN_COLS), f32) * D_MODEL ** -0.5
    conv_w = jax.random.normal(ks[3], (SHORT_CONV_WIDTH, D_CONV), f32) * SHORT_CONV_WIDTH ** -0.5
    lru_conv_w = jax.random.normal(ks[4], (LRU_CONV_WIDTH, D_LRU), f32) * LRU_CONV_WIDTH ** -0.5
    lru_conv_b = 0.02 * jax.random.normal(ks[5], (D_LRU,), f32)
    w_a = jax.random.normal(ks[6], (N_LRU_HEADS, LRU_HEAD_DIM, LRU_HEAD_DIM), f32) * LRU_HEAD_DIM ** -0.5
    b_a = 0.02 * jax.random.normal(ks[7], (D_LRU,), f32)
    w_i = jax.random.normal(ks[8], (N_LRU_HEADS, LRU_HEAD_DIM, LRU_HEAD_DIM), f32) * LRU_HEAD_DIM ** -0.5
    b_i = 0.02 * jax.random.normal(ks[9], (D_LRU,), f32)
    a_init = jax.random.uniform(ks[10], (D_LRU,), f32, minval=0.9, maxval=0.999)
    lam = jnp.log(a_init) - jnp.log1p(-a_init)
    conv_out_g = 1.0 + 0.02 * jax.random.normal(ks[11], (D_CONV,), f32)
    lru_out_g = 1.0 + 0.02 * jax.random.normal(ks[12], (D_LRU,), f32)
    w_out = jax.random.normal(ks[13], (D_MIX, D_MODEL), f32) * D_MIX ** -0.5
    final_g = 1.0 + 0.02 * jax.random.normal(ks[14], (D_MODEL,), f32)
    return {"x": x, "ln_g": ln_g, "w_in": w_in, "conv_w": conv_w,
            "lru_conv_w": lru_conv_w, "lru_conv_b": lru_conv_b,
            "w_a": w_a, "b_a": b_a, "w_i": w_i, "b_i": b_i, "lam": lam,
            "conv_out_g": conv_out_g, "lru_out_g": lru_out_g,
            "w_out": w_out, "final_g": final_g}


def reference(x, ln_g, w_in, conv_w, lru_conv_w, lru_conv_b, w_a, b_a, w_i, b_i,
              lam, conv_out_g, lru_out_g, w_out, final_g):
    h = x
    for _ in range(DEPTH):
        xn = rms_norm(h, ln_g)
        proj = jnp.einsum('bsd,de->bse', xn, w_in)
        splits = [D_CONV, 2 * D_CONV, 3 * D_CONV, 4 * D_CONV, 4 * D_CONV + D_LRU]
        b_gate, c_gate, x_conv, g_conv, x_lru, g_lru = jnp.split(proj, splits, axis=-1)
        y_conv = short_conv_mixer(b_gate, c_gate, x_conv, conv_w)
        y_conv = headwise_rms_norm(y_conv, N_CONV_HEADS, conv_out_g) * jax.nn.silu(g_conv)
        y_lru = rg_lru_mixer(x_lru, lru_conv_w, lru_conv_b, w_a, b_a, w_i, b_i, lam)
        y_lru = headwise_rms_norm(y_lru, N_LRU_HEADS, lru_out_g) * jax.nn.silu(g_lru)
        y = jnp.concatenate([y_conv, y_lru], axis=-1)
        h = h + jnp.einsum('bse,ed->bsd', y, w_out)
    return rms_norm(h, final_g)
```

```python
import functools

import jax
import jax.numpy as jnp
from jax import lax
from jax.experimental import pallas as pl
from jax.experimental.pallas import tpu as pltpu

SUBLANES = 8
LANES = 128
RMS_EPS = 1e-6
RG_LRU_C = 8.0
N_CONV_HEADS = 8
N_LRU_HEADS = 16
GATE_GROUP = 256
VMEM_LIMIT_BYTES = 56 * 1024 * 1024


def _sigmoid(v):
    return 1.0 / (1.0 + jnp.exp(-v))


def _layer_kernel(x_ref, ln_g_ref, w_in_ref, conv_w_ref, lconv_w_ref, lconv_b_ref,
                  wa_ref, wi_ref, b_a_ref, b_i_ref, c_lam_ref, conv_g_ref, lru_g_ref,
                  w_out_ref, final_g_ref, out_ref,
                  xn_s, proj_s, u32_s, ub_s, gates_s, h_s, ycat_s,
                  *, rows, d_model, d_conv, d_lru, hist):
    j = pl.program_id(0)
    f32 = jnp.float32
    bf16 = jnp.bfloat16
    n_groups = rows // SUBLANES
    col_b, col_c, col_x, col_g = 0, d_conv, 2 * d_conv, 3 * d_conv
    col_xl, col_gl = 4 * d_conv, 4 * d_conv + d_lru

    @pl.when(j == 0)
    def _():
        proj_s[pl.ds(0, hist), :] = jnp.zeros((hist, proj_s.shape[1]), f32)
        h_s[...] = jnp.zeros(h_s.shape, f32)

    rc_a = 32

    def stage_a(c, carry):
        r = pl.multiple_of(c * rc_a, rc_a)
        xv = x_ref[pl.ds(r, rc_a), :]
        ms = jnp.mean(xv * xv, axis=-1, keepdims=True)
        xn = (xv * lax.rsqrt(ms + RMS_EPS)) * ln_g_ref[0:1, :]
        xn_s[pl.ds(r, rc_a), :] = xn.astype(bf16)
        return carry

    lax.fori_loop(0, rows // rc_a, stage_a, 0)

    proj_s[pl.ds(hist, rows), :] = jnp.dot(
        xn_s[...], w_in_ref[...], preferred_element_type=f32)

    def stage_c1(g, carry):
        r = pl.multiple_of(g * SUBLANES, SUBLANES)
        cur = pl.ds(hist + r, SUBLANES)
        cg = proj_s[cur, col_c:col_c + d_conv]
        xc = proj_s[cur, col_x:col_x + d_conv]
        proj_s[cur, col_c:col_c + d_conv] = cg * xc
        u = lconv_b_ref[...]
        for k in range(4):
            rows_k = pl.ds(hist + r - SUBLANES * (3 - k), SUBLANES)
            u = u + proj_s[rows_k, col_xl:col_xl + d_lru] * lconv_w_ref[k]
        u32_s[pl.ds(r, SUBLANES), :] = u
        return carry

    lax.fori_loop(0, n_groups, stage_c1, 0)

    rc_b = 32

    def stage_c1b(c, carry):
        r = pl.multiple_of(c * rc_b, rc_b)
        ub_s[pl.ds(r, rc_b), :] = u32_s[pl.ds(r, rc_b), :].astype(bf16)
        return carry

    lax.fori_loop(0, rows // rc_b, stage_c1b, 0)

    for g in range(d_lru // GATE_GROUP):
        cols = slice(g * GATE_GROUP, (g + 1) * GATE_GROUP)
        ug = ub_s[:, cols]
        gates_s[:, cols] = jnp.dot(ug, wa_ref[g], preferred_element_type=f32)
        gates_s[:, d_lru + g * GATE_GROUP:d_lru + (g + 1) * GATE_GROUP] = jnp.dot(
            ug, wi_ref[g], preferred_element_type=f32)

    lane = lax.broadcasted_iota(jnp.int32, (SUBLANES, LANES), 1)
    low_half = lane < (LANES // 2)

    def one_position(r, h):
        cur = pl.ds(hist + r, SUBLANES)
        conv = proj_s[cur, col_c:col_c + d_conv] * conv_w_ref[2]
        conv = conv + proj_s[pl.ds(hist + r - SUBLANES, SUBLANES), col_c:col_c + d_conv] * conv_w_ref[1]
        conv = conv + proj_s[pl.ds(hist + r - 2 * SUBLANES, SUBLANES), col_c:col_c + d_conv] * conv_w_ref[0]
        y = proj_s[cur, col_b:col_b + d_conv] * conv
        gate = proj_s[cur, col_g:col_g + d_conv]
        gate = gate * _sigmoid(gate)
        parts = []
        for hd in range(N_CONV_HEADS):
            yh = y[:, hd * LANES:(hd + 1) * LANES]
            ms = jnp.mean(yh * yh, axis=-1, keepdims=True)
            parts.append(yh * lax.rsqrt(ms + RMS_EPS))
        y_conv = (jnp.concatenate(parts, axis=-1) * conv_g_ref[...]) * gate

        u = u32_s[pl.ds(r, SUBLANES), :]
        rg = _sigmoid(gates_s[pl.ds(r, SUBLANES), 0:d_lru] + b_a_ref[...])
        ig = _sigmoid(gates_s[pl.ds(r, SUBLANES), d_lru:2 * d_lru] + b_i_ref[...])
        log_a = rg * c_lam_ref[...]
        a = jnp.exp(log_a)
        mult = jnp.sqrt(-jnp.tanh(log_a) * (a * a + 1.0))
        h = a * h + mult * (ig * u)
        gl = proj_s[cur, col_gl:col_gl + d_lru]
        gl = gl * _sigmoid(gl)
        parts = []
        inv_n = 1.0 / (LANES // 2)
        for v in range(d_lru // LANES):
            hv = h[:, v * LANES:(v + 1) * LANES]
            sq = hv * hv
            ms_lo = jnp.sum(jnp.where(low_half, sq, 0.0), axis=-1, keepdims=True) * inv_n
            ms_hi = jnp.sum(jnp.where(low_half, 0.0, sq), axis=-1, keepdims=True) * inv_n
            rstd = jnp.where(low_half, lax.rsqrt(ms_lo + RMS_EPS), lax.rsqrt(ms_hi + RMS_EPS))
            parts.append(hv * rstd)
        y_lru = (jnp.concatenate(parts, axis=-1) * lru_g_ref[...]) * gl
        return y_conv, y_lru, h

    pos_per_iter = 2

    def stage_c3(g, h):
        r0 = pl.multiple_of(g * (pos_per_iter * SUBLANES), pos_per_iter * SUBLANES)
        ycs, yls = [], []
        for p in range(pos_per_iter):
            yc, yl, h = one_position(r0 + p * SUBLANES, h)
            ycs.append(yc)
            yls.append(yl)
        out_rows = pl.ds(r0, pos_per_iter * SUBLANES)
        ycat_s[out_rows, 0:d_conv] = jnp.concatenate(ycs, axis=0).astype(bf16)
        ycat_s[out_rows, d_conv:d_conv + d_lru] = jnp.concatenate(yls, axis=0).astype(bf16)
        return h

    h_fin = lax.fori_loop(0, n_groups // pos_per_iter, stage_c3, h_s[...])
    h_s[...] = h_fin

    proj_s[pl.ds(0, hist), :] = proj_s[pl.ds(rows, hist), :]

    out_ref[...] = jnp.dot(ycat_s[...], w_out_ref[...], preferred_element_type=f32)

    rc_d = 32

    def stage_d(c, carry):
        r = pl.multiple_of(c * rc_d, rc_d)
        hres = x_ref[pl.ds(r, rc_d), :] + out_ref[pl.ds(r, rc_d), :]
        ms = jnp.mean(hres * hres, axis=-1, keepdims=True)
        out_ref[pl.ds(r, rc_d), :] = (hres * lax.rsqrt(ms + RMS_EPS)) * final_g_ref[0:1, :]
        return carry

    lax.fori_loop(0, rows // rc_d, stage_d, 0)


def _block_diag_groups(w, group):
    heads, hd, _ = w.shape
    per = group // hd
    wg = w.reshape(heads // per, per, hd, hd)
    eye = jnp.eye(per, dtype=w.dtype)
    dense = jnp.einsum('gpde,pq->gpdqe', wg, eye)
    return dense.reshape(heads // per, group, group)


def _rows8(v):
    return jnp.broadcast_to(v.reshape(1, -1), (SUBLANES, v.shape[-1]))


@functools.partial(jax.jit, static_argnames=("t_step",))
def _forward(x, ln_g, w_in, conv_w, lru_conv_w, lru_conv_b, w_a, b_a, w_i, b_i,
             lam, conv_out_g, lru_out_g, w_out, final_g, t_step=32):
    bsz, seq, d_model = x.shape
    d_conv = conv_w.shape[1]
    d_lru = lru_conv_w.shape[1]
    in_cols = w_in.shape[1]
    assert bsz == SUBLANES and in_cols == 4 * d_conv + 2 * d_lru
    assert seq % t_step == 0 and d_lru % GATE_GROUP == 0
    rows = t_step * SUBLANES
    hist = (lru_conv_w.shape[0] - 1) * SUBLANES
    bf16 = jnp.bfloat16
    f32 = jnp.float32

    xt = jnp.transpose(x, (1, 0, 2)).reshape(seq * bsz, d_model)
    c_lam = RG_LRU_C * jax.nn.log_sigmoid(lam.astype(f32))
    wa_g = _block_diag_groups(w_a, GATE_GROUP).astype(bf16)
    wi_g = _block_diag_groups(w_i, GATE_GROUP).astype(bf16)
    conv_w8 = jnp.broadcast_to(conv_w[:, None, :], (conv_w.shape[0], SUBLANES, d_conv))
    lconv_w8 = jnp.broadcast_to(lru_conv_w[:, None, :], (lru_conv_w.shape[0], SUBLANES, d_lru))

    def const(shape):
        return pl.BlockSpec(shape, lambda j: (0,) * len(shape), pipeline_mode=pl.Buffered(1))

    kern = functools.partial(_layer_kernel, rows=rows, d_model=d_model, d_conv=d_conv,
                             d_lru=d_lru, hist=hist)
    out2d = pl.pallas_call(
        kern,
        grid=(seq // t_step,),
        in_specs=[
            pl.BlockSpec((rows, d_model), lambda j: (j, 0)),
            const((1, d_model)),
            const((d_model, in_cols)),
            const(conv_w8.shape),
            const(lconv_w8.shape),
            const((SUBLANES, d_lru)),
            const(wa_g.shape),
            const(wi_g.shape),
            const((SUBLANES, d_lru)),
            const((SUBLANES, d_lru)),
            const((SUBLANES, d_lru)),
            const((SUBLANES, d_conv)),
            const((SUBLANES, d_lru)),
            const((d_conv + d_lru, d_model)),
            const((1, d_model)),
        ],
        out_specs=pl.BlockSpec((rows, d_model), lambda j: (j, 0)),
        out_shape=jax.ShapeDtypeStruct((seq * bsz, d_model), f32),
        scratch_shapes=[
            pltpu.VMEM((rows, d_model), bf16),
            pltpu.VMEM((rows + hist, in_cols), f32),
            pltpu.VMEM((rows, d_lru), f32),
            pltpu.VMEM((rows, d_lru), bf16),
            pltpu.VMEM((rows, 2 * d_lru), f32),
            pltpu.VMEM((SUBLANES, d_lru), f32),
            pltpu.VMEM((rows, d_conv + d_lru), bf16),
        ],
        compiler_params=pltpu.CompilerParams(
            dimension_semantics=("arbitrary",),
            vmem_limit_bytes=VMEM_LIMIT_BYTES),
        name="hybrid_layer",
    )(xt, ln_g.reshape(1, d_model), w_in.astype(bf16), conv_w8, lconv_w8,
      _rows8(lru_conv_b), wa_g, wi_g, _rows8(b_a), _rows8(b_i), _rows8(c_lam),
      _rows8(conv_out_g), _rows8(lru_out_g), w_out.astype(bf16),
      final_g.reshape(1, d_model))
    return jnp.transpose(out2d.reshape(seq, bsz, d_model), (1, 0, 2))


def kernel(x, ln_g, w_in, conv_w, lru_conv_w, lru_conv_b, w_a, b_a, w_i, b_i, lam,
           conv_out_g, lru_out_g, w_out, final_g):
    return _forward(x, ln_g, w_in, conv_w, lru_conv_w, lru_conv_b, w_a, b_a, w_i, b_i,
                    lam, conv_out_g, lru_out_g, w_out, final_g)
```

```python
import functools

import jax
import jax.numpy as jnp
from jax import lax
from jax.experimental import pallas as pl
from jax.experimental.pallas import tpu as pltpu

SUBLANES = 8
LANES = 128
BF16_ROWS = 16
RMS_EPS = 1e-6
RG_LRU_C = 8.0
N_CONV_HEADS = 8
N_LRU_HEADS = 16
GATE_GROUP = 256
NORM_ROWS = 32
VMEM_LIMIT_BYTES = 58 * 1024 * 1024


def _sigmoid(v):
    return 1.0 / (1.0 + jnp.exp(-v))


def _rms_rows(v, g_row):
    ms = jnp.mean(v * v, axis=-1, keepdims=True)
    return (v * lax.rsqrt(ms + RMS_EPS)) * g_row


def _layer_kernel(x_next_ref, x_prev_ref, ln_g_ref, w_in_ref, conv_w_ref, lconv_w_ref,
                  lconv_b_ref, wa_ref, wi_ref, b_a_ref, b_i_ref, c_lam_ref, conv_g_ref,
                  lru_g_ref, w_out_ref, final_g_ref, out_ref,
                  proj_a, proj_b, u_a, u_b, ycat_a, ycat_b, xn_s, ub_s, gates_s, h_s,
                  *, rows, d_conv, d_lru, hist):
    s = pl.program_id(0)
    f32 = jnp.float32
    bf16 = jnp.bfloat16
    n_pos = rows // SUBLANES
    col_b, col_c, col_x, col_g = 0, d_conv, 2 * d_conv, 3 * d_conv
    col_xl, col_gl = 4 * d_conv, 4 * d_conv + d_lru
    lane = lax.broadcasted_iota(jnp.int32, (SUBLANES, LANES), 1)
    low_half = lane < (LANES // 2)

    @pl.when(s == 0)
    def _():
        for ref in (proj_b, u_b, ub_s, ycat_a, h_s):
            ref[...] = jnp.zeros(ref.shape, ref.dtype)

    def stage_in(row0, proj_w, proj_r, u_w):
        for blk in range(rows // NORM_ROWS):
            rs = slice(blk * NORM_ROWS, (blk + 1) * NORM_ROWS)
            xs = slice(row0 + blk * NORM_ROWS, row0 + (blk + 1) * NORM_ROWS)
            xn_s[rs, :] = _rms_rows(x_next_ref[xs, :], ln_g_ref[0:1, :]).astype(bf16)
        proj_w[pl.ds(hist, rows), :] = jnp.dot(
            xn_s[...], w_in_ref[...], preferred_element_type=f32)
        for col, width in ((col_c, d_conv), (col_xl, d_lru)):
            proj_w[0:hist, col:col + width] = proj_r[rows:rows + hist, col:col + width]
        for p in range(n_pos):
            cur = slice(hist + p * SUBLANES, hist + (p + 1) * SUBLANES)
            proj_w[cur, col_c:col_c + d_conv] = (
                proj_w[cur, col_c:col_c + d_conv] * proj_w[cur, col_x:col_x + d_conv])
            u = lconv_b_ref[...]
            for k in range(4):
                back = slice(hist + (p - 3 + k) * SUBLANES, hist + (p - 2 + k) * SUBLANES)
                u = u + proj_w[back, col_xl:col_xl + d_lru] * lconv_w_ref[k]
            u_w[p * SUBLANES:(p + 1) * SUBLANES, :] = u
        for blk in range(rows // NORM_ROWS):
            rs = slice(blk * NORM_ROWS, (blk + 1) * NORM_ROWS)
            ub_s[rs, :] = u_w[rs, :].astype(bf16)

    def gate_maps():
        for g in range(d_lru // GATE_GROUP):
            cols = slice(g * GATE_GROUP, (g + 1) * GATE_GROUP)
            ug = ub_s[:, cols]
            gates_s[:, cols] = jnp.dot(ug, wa_ref[g], preferred_element_type=f32)
            gates_s[:, d_lru + g * GATE_GROUP:d_lru + (g + 1) * GATE_GROUP] = jnp.dot(
                ug, wi_ref[g], preferred_element_type=f32)

    def one_position(p, h, proj_r, u_r):
        cur = slice(hist + p * SUBLANES, hist + (p + 1) * SUBLANES)
        rows_p = slice(p * SUBLANES, (p + 1) * SUBLANES)
        conv = proj_r[cur, col_c:col_c + d_conv] * conv_w_ref[2]
        for k in (1, 0):
            back = slice(hist + (p - 2 + k) * SUBLANES, hist + (p - 1 + k) * SUBLANES)
            conv = conv + proj_r[back, col_c:col_c + d_conv] * conv_w_ref[k]
        y = proj_r[cur, col_b:col_b + d_conv] * conv
        gate = proj_r[cur, col_g:col_g + d_conv]
        gate = gate * _sigmoid(gate)
        parts = []
        for hd in range(N_CONV_HEADS):
            yh = y[:, hd * LANES:(hd + 1) * LANES]
            ms = jnp.mean(yh * yh, axis=-1, keepdims=True)
            parts.append(yh * lax.rsqrt(ms + RMS_EPS))
        y_conv = (jnp.concatenate(parts, axis=-1) * conv_g_ref[...]) * gate

        u = u_r[rows_p, :]
        rg = _sigmoid(gates_s[rows_p, 0:d_lru] + b_a_ref[...])
        ig = _sigmoid(gates_s[rows_p, d_lru:2 * d_lru] + b_i_ref[...])
        log_a = rg * c_lam_ref[...]
        a = jnp.exp(log_a)
        mult = jnp.sqrt(-jnp.tanh(log_a) * (a * a + 1.0))
        h = a * h + mult * (ig * u)
        gl = proj_r[cur, col_gl:col_gl + d_lru]
        gl = gl * _sigmoid(gl)
        parts = []
        inv_n = 1.0 / (LANES // 2)
        for v in range(d_lru // LANES):
            hv = h[:, v * LANES:(v + 1) * LANES]
            sq = hv * hv
            ms_lo = jnp.sum(jnp.where(low_half, sq, 0.0), axis=-1, keepdims=True) * inv_n
            ms_hi = jnp.sum(jnp.where(low_half, 0.0, sq), axis=-1, keepdims=True) * inv_n
            rstd = jnp.where(low_half, lax.rsqrt(ms_lo + RMS_EPS), lax.rsqrt(ms_hi + RMS_EPS))
            parts.append(hv * rstd)
        y_lru = (jnp.concatenate(parts, axis=-1) * lru_g_ref[...]) * gl
        return y_conv, y_lru, h

    def stage_mix(proj_r, u_r, ycat_w, h):
        per = BF16_ROWS // SUBLANES
        for q in range(n_pos // per):
            ycs, yls = [], []
            for p in range(q * per, (q + 1) * per):
                yc, yl, h = one_position(p, h, proj_r, u_r)
                ycs.append(yc)
                yls.append(yl)
            out_rows = slice(q * BF16_ROWS, (q + 1) * BF16_ROWS)
            ycat_w[out_rows, 0:d_conv] = jnp.concatenate(ycs, axis=0).astype(bf16)
            ycat_w[out_rows, d_conv:d_conv + d_lru] = jnp.concatenate(yls, axis=0).astype(bf16)
        return h

    def stage_out(ycat_r, row0):
        out_ref[row0:row0 + rows, :] = jnp.dot(
            ycat_r[...], w_out_ref[...], preferred_element_type=f32)
        for blk in range(rows // NORM_ROWS):
            rs = slice(row0 + blk * NORM_ROWS, row0 + (blk + 1) * NORM_ROWS)
            out_ref[rs, :] = _rms_rows(x_prev_ref[rs, :] + out_ref[rs, :], final_g_ref[0:1, :])

    def half(row0, proj_w, proj_r, u_w, u_r, ycat_w, ycat_r, h):
        gate_maps()
        h = stage_mix(proj_r, u_r, ycat_w, h)
        stage_in(row0, proj_w, proj_r, u_w)
        stage_out(ycat_r, row0)
        return h

    h = half(0, proj_a, proj_b, u_a, u_b, ycat_b, ycat_a, h_s[...])
    h = jnp.where(s == 0, jnp.zeros_like(h), h)
    h = half(rows, proj_b, proj_a, u_b, u_a, ycat_a, ycat_b, h)
    h_s[...] = h


def _block_diag_groups(w, group):
    heads, hd, _ = w.shape
    per = group // hd
    wg = w.reshape(heads // per, per, hd, hd)
    eye = jnp.eye(per, dtype=w.dtype)
    dense = jnp.einsum('gpde,pq->gpdqe', wg, eye)
    return dense.reshape(heads // per, group, group)


def _rows8(v):
    return jnp.broadcast_to(v.reshape(1, -1), (SUBLANES, v.shape[-1]))


@functools.partial(jax.jit, static_argnames=("chunk_pos",))
def _forward(x, ln_g, w_in, conv_w, lru_conv_w, lru_conv_b, w_a, b_a, w_i, b_i,
             lam, conv_out_g, lru_out_g, w_out, final_g, chunk_pos=32):
    bsz, seq, d_model = x.shape
    d_conv = conv_w.shape[1]
    d_lru = lru_conv_w.shape[1]
    in_cols = w_in.shape[1]
    assert bsz == SUBLANES and in_cols == 4 * d_conv + 2 * d_lru
    assert seq % (2 * chunk_pos) == 0 and d_lru % GATE_GROUP == 0
    rows = chunk_pos * SUBLANES
    hist = (lru_conv_w.shape[0] - 1) * SUBLANES
    n_steps = seq // (2 * chunk_pos)
    bf16 = jnp.bfloat16
    f32 = jnp.float32

    xt = jnp.transpose(x, (1, 0, 2)).reshape(seq * bsz, d_model)
    c_lam = RG_LRU_C * jax.nn.log_sigmoid(lam.astype(f32))
    wa_g = _block_diag_groups(w_a, GATE_GROUP).astype(bf16)
    wi_g = _block_diag_groups(w_i, GATE_GROUP).astype(bf16)
    conv_w8 = jnp.broadcast_to(conv_w[:, None, :], (conv_w.shape[0], SUBLANES, d_conv))
    lconv_w8 = jnp.broadcast_to(lru_conv_w[:, None, :], (lru_conv_w.shape[0], SUBLANES, d_lru))

    def const(shape):
        return pl.BlockSpec(shape, lambda s: (0,) * len(shape), pipeline_mode=pl.Buffered(1))

    kern = functools.partial(_layer_kernel, rows=rows, d_conv=d_conv, d_lru=d_lru, hist=hist)
    out2d = pl.pallas_call(
        kern,
        grid=(n_steps + 1,),
        in_specs=[
            pl.BlockSpec((2 * rows, d_model), lambda s: (jnp.minimum(s, n_steps - 1), 0)),
            pl.BlockSpec((2 * rows, d_model), lambda s: (jnp.maximum(s - 1, 0), 0)),
            const((1, d_model)),
            const((d_model, in_cols)),
            const(conv_w8.shape),
            const(lconv_w8.shape),
            const((SUBLANES, d_lru)),
            const(wa_g.shape),
            const(wi_g.shape),
            const((SUBLANES, d_lru)),
            const((SUBLANES, d_lru)),
            const((SUBLANES, d_lru)),
            const((SUBLANES, d_conv)),
            const((SUBLANES, d_lru)),
            const((d_conv + d_lru, d_model)),
            const((1, d_model)),
        ],
        out_specs=pl.BlockSpec((2 * rows, d_model), lambda s: (jnp.maximum(s - 1, 0), 0)),
        out_shape=jax.ShapeDtypeStruct((seq * bsz, d_model), f32),
        scratch_shapes=[
            pltpu.VMEM((rows + hist, in_cols), f32),
            pltpu.VMEM((rows + hist, in_cols), f32),
            pltpu.VMEM((rows, d_lru), f32),
            pltpu.VMEM((rows, d_lru), f32),
            pltpu.VMEM((rows, d_conv + d_lru), bf16),
            pltpu.VMEM((rows, d_conv + d_lru), bf16),
            pltpu.VMEM((rows, d_model), bf16),
            pltpu.VMEM((rows, d_lru), bf16),
            pltpu.VMEM((rows, 2 * d_lru), f32),
            pltpu.VMEM((SUBLANES, d_lru), f32),
        ],
        compiler_params=pltpu.CompilerParams(
            dimension_semantics=("arbitrary",),
            vmem_limit_bytes=VMEM_LIMIT_BYTES),
        name="hybrid_layer",
    )(xt, xt, ln_g.reshape(1, d_model), w_in.astype(bf16), conv_w8, lconv_w8,
      _rows8(lru_conv_b), wa_g, wi_g, _rows8(b_a), _rows8(b_i), _rows8(c_lam),
      _rows8(conv_out_g), _rows8(lru_out_g), w_out.astype(bf16),
      final_g.reshape(1, d_model))
    return jnp.transpose(out2d.reshape(seq, bsz, d_model), (1, 0, 2))


def kernel(x, ln_g, w_in, conv_w, lru_conv_w, lru_conv_b, w_a, b_a, w_i, b_i, lam,
           conv_out_g, lru_out_g, w_out, final_g):
    return _forward(x, ln_g, w_in, conv_w, lru_conv_w, lru_conv_b, w_a, b_a, w_i, b_i,
                    lam, conv_out_g, lru_out_g, w_out, final_g)
```

```python
import functools

import jax
import jax.numpy as jnp
from jax import lax
from jax.experimental import pallas as pl
from jax.experimental.pallas import tpu as pltpu

SUBLANES = 8
LANES = 128
BF16_ROWS = 16
RMS_EPS = 1e-6
RG_LRU_C = 8.0
N_CONV_HEADS = 8
N_LRU_HEADS = 16
GATE_GROUP = 256
NORM_ROWS = 32
X_SLOTS = 3
O_SLOTS = 2
VMEM_LIMIT_BYTES = 58 * 1024 * 1024


def _sigmoid(v):
    return 1.0 / (1.0 + jnp.exp(-v))


def _rms_rows(v, g_rows):
    ms = jnp.mean(v * v, axis=-1, keepdims=True)
    return (v * lax.rsqrt(ms + RMS_EPS)) * g_rows


def _layer_kernel(x_hbm, ln_g_ref, w_in_ref, conv_w_ref, lconv_w_ref,
                  lconv_b_ref, wa_ref, wi_ref, b_a_ref, b_i_ref, c_lam_ref, conv_g_ref,
                  lru_g_ref, w_out_ref, final_g_ref, out_hbm,
                  proj_a, proj_b, u_a, u_b, ycat_a, ycat_b, xn_s, ub_s, gates_s, h_s,
                  acc_s, xbuf, obuf, in_sem, out_sem,
                  *, rows, d_conv, d_lru, hist, n_steps):
    s = pl.program_id(0)
    f32 = jnp.float32
    bf16 = jnp.bfloat16
    n_batch = xbuf.shape[1]
    n_pos = rows // SUBLANES
    blk_pos = 2 * n_pos
    col_b, col_c, col_x, col_g = 0, d_conv, 2 * d_conv, 3 * d_conv
    col_xl, col_gl = 4 * d_conv, 4 * d_conv + d_lru
    lane = lax.broadcasted_iota(jnp.int32, (SUBLANES, LANES), 1)
    low_half = lane < (LANES // 2)

    def x_copy(blk, b):
        slot = blk % X_SLOTS
        return pltpu.make_async_copy(
            x_hbm.at[b, pl.ds(blk * blk_pos, blk_pos), :],
            xbuf.at[pl.ds(slot * blk_pos, blk_pos), b, :],
            in_sem.at[slot])

    def out_copy(blk, b):
        slot = blk % O_SLOTS
        return pltpu.make_async_copy(
            obuf.at[pl.ds(slot * blk_pos, blk_pos), b, :],
            out_hbm.at[b, pl.ds(blk * blk_pos, blk_pos), :],
            out_sem.at[slot])

    def start_all(copy, blk):
        for b in range(n_batch):
            copy(blk, b).start()

    def wait_all(copy, blk):
        for b in range(n_batch):
            copy(blk, b).wait()

    @pl.when(s == 0)
    def _():
        for ref in (proj_b, u_b, ub_s, ycat_a, h_s):
            ref[...] = jnp.zeros(ref.shape, ref.dtype)
        start_all(x_copy, 0)

    @pl.when(s + 1 < n_steps)
    def _():
        start_all(x_copy, s + 1)

    @pl.when(s < n_steps)
    def _():
        wait_all(x_copy, s)

    @pl.when(s >= O_SLOTS + 1)
    def _():
        wait_all(out_copy, s - 1 - O_SLOTS)

    x_in0 = (jnp.minimum(s, n_steps - 1) % X_SLOTS) * blk_pos
    x_res0 = (jnp.maximum(s - 1, 0) % X_SLOTS) * blk_pos
    o_out0 = ((s + 1) % O_SLOTS) * blk_pos

    def stage_in(pos0, proj_w, proj_r, u_w):
        per = BF16_ROWS // SUBLANES
        for q in range(n_pos // per):
            xn = [_rms_rows(xbuf[x_in0 + pos0 + p], ln_g_ref[...])
                  for p in range(q * per, (q + 1) * per)]
            xn_s[q * BF16_ROWS:(q + 1) * BF16_ROWS, :] = jnp.concatenate(xn, axis=0).astype(bf16)
        proj_w[pl.ds(hist, rows), :] = jnp.dot(
            xn_s[...], w_in_ref[...], preferred_element_type=f32)
        for col, width in ((col_c, d_conv), (col_xl, d_lru)):
            proj_w[0:hist, col:col + width] = proj_r[rows:rows + hist, col:col + width]
        for p in range(n_pos):
            cur = slice(hist + p * SUBLANES, hist + (p + 1) * SUBLANES)
            proj_w[cur, col_c:col_c + d_conv] = (
                proj_w[cur, col_c:col_c + d_conv] * proj_w[cur, col_x:col_x + d_conv])
            u = lconv_b_ref[...]
            for k in range(4):
                back = slice(hist + (p - 3 + k) * SUBLANES, hist + (p - 2 + k) * SUBLANES)
                u = u + proj_w[back, col_xl:col_xl + d_lru] * lconv_w_ref[k]
            u_w[p * SUBLANES:(p + 1) * SUBLANES, :] = u
        for blk in range(rows // NORM_ROWS):
            rs = slice(blk * NORM_ROWS, (blk + 1) * NORM_ROWS)
            ub_s[rs, :] = u_w[rs, :].astype(bf16)

    def gate_maps():
        for g in range(d_lru // GATE_GROUP):
            cols = slice(g * GATE_GROUP, (g + 1) * GATE_GROUP)
            ug = ub_s[:, cols]
            gates_s[:, cols] = jnp.dot(ug, wa_ref[g], preferred_element_type=f32)
            gates_s[:, d_lru + g * GATE_GROUP:d_lru + (g + 1) * GATE_GROUP] = jnp.dot(
                ug, wi_ref[g], preferred_element_type=f32)

    def one_position(p, h, proj_r, u_r):
        cur = slice(hist + p * SUBLANES, hist + (p + 1) * SUBLANES)
        rows_p = slice(p * SUBLANES, (p + 1) * SUBLANES)
        conv = proj_r[cur, col_c:col_c + d_conv] * conv_w_ref[2]
        for k in (1, 0):
            back = slice(hist + (p - 2 + k) * SUBLANES, hist + (p - 1 + k) * SUBLANES)
            conv = conv + proj_r[back, col_c:col_c + d_conv] * conv_w_ref[k]
        y = proj_r[cur, col_b:col_b + d_conv] * conv
        gate = proj_r[cur, col_g:col_g + d_conv]
        gate = gate * _sigmoid(gate)
        parts = []
        for hd in range(N_CONV_HEADS):
            yh = y[:, hd * LANES:(hd + 1) * LANES]
            ms = jnp.mean(yh * yh, axis=-1, keepdims=True)
            parts.append(yh * lax.rsqrt(ms + RMS_EPS))
        y_conv = (jnp.concatenate(parts, axis=-1) * conv_g_ref[...]) * gate

        u = u_r[rows_p, :]
        rg = _sigmoid(gates_s[rows_p, 0:d_lru] + b_a_ref[...])
        ig = _sigmoid(gates_s[rows_p, d_lru:2 * d_lru] + b_i_ref[...])
        log_a = rg * c_lam_ref[...]
        a = jnp.exp(log_a)
        mult = jnp.sqrt(-jnp.tanh(log_a) * (a * a + 1.0))
        h = a * h + mult * (ig * u)
        gl = proj_r[cur, col_gl:col_gl + d_lru]
        gl = gl * _sigmoid(gl)
        parts = []
        inv_n = 1.0 / (LANES // 2)
        for v in range(d_lru // LANES):
            hv = h[:, v * LANES:(v + 1) * LANES]
            sq = hv * hv
            ms_lo = jnp.sum(jnp.where(low_half, sq, 0.0), axis=-1, keepdims=True) * inv_n
            ms_hi = jnp.sum(jnp.where(low_half, 0.0, sq), axis=-1, keepdims=True) * inv_n
            rstd = jnp.where(low_half, lax.rsqrt(ms_lo + RMS_EPS), lax.rsqrt(ms_hi + RMS_EPS))
            parts.append(hv * rstd)
        y_lru = (jnp.concatenate(parts, axis=-1) * lru_g_ref[...]) * gl
        return y_conv, y_lru, h

    def stage_mix(proj_r, u_r, ycat_w, h):
        per = BF16_ROWS // SUBLANES
        for q in range(n_pos // per):
            ycs, yls = [], []
            for p in range(q * per, (q + 1) * per):
                yc, yl, h = one_position(p, h, proj_r, u_r)
                ycs.append(yc)
                yls.append(yl)
            out_rows = slice(q * BF16_ROWS, (q + 1) * BF16_ROWS)
            ycat_w[out_rows, 0:d_conv] = jnp.concatenate(ycs, axis=0).astype(bf16)
            ycat_w[out_rows, d_conv:d_conv + d_lru] = jnp.concatenate(yls, axis=0).astype(bf16)
        return h

    def stage_out(ycat_r, pos0):
        acc_s[...] = jnp.dot(ycat_r[...], w_out_ref[...], preferred_element_type=f32)
        for p in range(n_pos):
            hres = xbuf[x_res0 + pos0 + p] + acc_s[p * SUBLANES:(p + 1) * SUBLANES, :]
            obuf[o_out0 + pos0 + p] = _rms_rows(hres, final_g_ref[...])

    def half(pos0, proj_w, proj_r, u_w, u_r, ycat_w, ycat_r, h):
        gate_maps()
        h = stage_mix(proj_r, u_r, ycat_w, h)
        stage_in(pos0, proj_w, proj_r, u_w)
        stage_out(ycat_r, pos0)
        return h

    h = half(0, proj_a, proj_b, u_a, u_b, ycat_b, ycat_a, h_s[...])
    h = jnp.where(s == 0, jnp.zeros_like(h), h)
    h = half(n_pos, proj_b, proj_a, u_b, u_a, ycat_a, ycat_b, h)
    h_s[...] = h

    @pl.when(s >= 1)
    def _():
        start_all(out_copy, s - 1)

    @pl.when(s == n_steps)
    def _():
        for back in range(O_SLOTS, 0, -1):
            wait_all(out_copy, s - back)


def _block_diag_groups(w, group):
    heads, hd, _ = w.shape
    per = group // hd
    wg = w.reshape(heads // per, per, hd, hd)
    eye = jnp.eye(per, dtype=w.dtype)
    dense = jnp.einsum('gpde,pq->gpdqe', wg, eye)
    return dense.reshape(heads // per, group, group)


def _rows8(v):
    return jnp.broadcast_to(v.reshape(1, -1), (SUBLANES, v.shape[-1]))


@functools.partial(jax.jit, static_argnames=("chunk_pos",))
def _forward(x, ln_g, w_in, conv_w, lru_conv_w, lru_conv_b, w_a, b_a, w_i, b_i,
             lam, conv_out_g, lru_out_g, w_out, final_g, chunk_pos=32):
    bsz, seq, d_model = x.shape
    d_conv = conv_w.shape[1]
    d_lru = lru_conv_w.shape[1]
    in_cols = w_in.shape[1]
    assert bsz == SUBLANES and in_cols == 4 * d_conv + 2 * d_lru
    assert seq % (2 * chunk_pos) == 0 and d_lru % GATE_GROUP == 0
    rows = chunk_pos * SUBLANES
    hist = (lru_conv_w.shape[0] - 1) * SUBLANES
    n_steps = seq // (2 * chunk_pos)
    assert n_steps > O_SLOTS
    blk_pos = 2 * chunk_pos
    bf16 = jnp.bfloat16
    f32 = jnp.float32

    c_lam = RG_LRU_C * jax.nn.log_sigmoid(lam.astype(f32))
    wa_g = _block_diag_groups(w_a, GATE_GROUP).astype(bf16)
    wi_g = _block_diag_groups(w_i, GATE_GROUP).astype(bf16)
    conv_w8 = jnp.broadcast_to(conv_w[:, None, :], (conv_w.shape[0], SUBLANES, d_conv))
    lconv_w8 = jnp.broadcast_to(lru_conv_w[:, None, :], (lru_conv_w.shape[0], SUBLANES, d_lru))

    def const(shape):
        return pl.BlockSpec(shape, lambda s: (0,) * len(shape), pipeline_mode=pl.Buffered(1))

    kern = functools.partial(_layer_kernel, rows=rows, d_conv=d_conv, d_lru=d_lru, hist=hist,
                             n_steps=n_steps)
    return pl.pallas_call(
        kern,
        grid=(n_steps + 1,),
        in_specs=[
            pl.BlockSpec(memory_space=pl.ANY),
            const((SUBLANES, d_model)),
            const((d_model, in_cols)),
            const(conv_w8.shape),
            const(lconv_w8.shape),
            const((SUBLANES, d_lru)),
            const(wa_g.shape),
            const(wi_g.shape),
            const((SUBLANES, d_lru)),
            const((SUBLANES, d_lru)),
            const((SUBLANES, d_lru)),
            const((SUBLANES, d_conv)),
            const((SUBLANES, d_lru)),
            const((d_conv + d_lru, d_model)),
            const((SUBLANES, d_model)),
        ],
        out_specs=pl.BlockSpec(memory_space=pl.ANY),
        out_shape=jax.ShapeDtypeStruct((bsz, seq, d_model), f32),
        scratch_shapes=[
            pltpu.VMEM((rows + hist, in_cols), f32),
            pltpu.VMEM((rows + hist, in_cols), f32),
            pltpu.VMEM((rows, d_lru), f32),
            pltpu.VMEM((rows, d_lru), f32),
            pltpu.VMEM((rows, d_conv + d_lru), bf16),
            pltpu.VMEM((rows, d_conv + d_lru), bf16),
            pltpu.VMEM((rows, d_model), bf16),
            pltpu.VMEM((rows, d_lru), bf16),
            pltpu.VMEM((rows, 2 * d_lru), f32),
            pltpu.VMEM((SUBLANES, d_lru), f32),
            pltpu.VMEM((rows, d_model), f32),
            pltpu.VMEM((X_SLOTS * blk_pos, bsz, d_model), f32),
            pltpu.VMEM((O_SLOTS * blk_pos, bsz, d_model), f32),
            pltpu.SemaphoreType.DMA((X_SLOTS,)),
            pltpu.SemaphoreType.DMA((O_SLOTS,)),
        ],
        compiler_params=pltpu.CompilerParams(
            dimension_semantics=("arbitrary",),
            vmem_limit_bytes=VMEM_LIMIT_BYTES),
        name="hybrid_layer",
    )(x, _rows8(ln_g), w_in.astype(bf16), conv_w8, lconv_w8,
      _rows8(lru_conv_b), wa_g, wi_g, _rows8(b_a), _rows8(b_i), _rows8(c_lam),
      _rows8(conv_out_g), _rows8(lru_out_g), w_out.astype(bf16), _rows8(final_g))


def kernel(x, ln_g, w_in, conv_w, lru_conv_w, lru_conv_b, w_a, b_a, w_i, b_i, lam,
           conv_out_g, lru_out_g, w_out, final_g):
    return _forward(x, ln_g, w_in, conv_w, lru_conv_w, lru_conv_b, w_a, b_a, w_i, b_i,
                    lam, conv_out_g, lru_out_g, w_out, final_g)
```

```python
import functools

import jax
import jax.numpy as jnp
from jax import lax
from jax.experimental import pallas as pl
from jax.experimental.pallas import tpu as pltpu

SUBLANES = 8
LANES = 128
BF16_ROWS = 16
RMS_EPS = 1e-6
RG_LRU_C = 8.0
N_CONV_HEADS = 8
N_LRU_HEADS = 16
GATE_GROUP = 256
MXU_COLS = 256
NORM_ROWS = 32
X_SLOTS = 3
O_SLOTS = 2
VMEM_LIMIT_BYTES = 58 * 1024 * 1024


def _silu_from_half(hg):
    return hg + hg * jnp.tanh(hg)


def _rms_rows(v, g_rows):
    ms = jnp.mean(v * v, axis=-1, keepdims=True)
    return (v * lax.rsqrt(ms + RMS_EPS)) * g_rows


def _layer_kernel(x_hbm, ln_g_ref, w_in_ref, conv_w_ref, lconv_w_ref,
                  lconv_b_ref, wa_ref, wi_ref, b_a_ref, b_i_ref, c_lam_ref, conv_g_ref,
                  lru_g_ref, w_out_ref, final_g_ref, out_hbm,
                  proj_a, proj_b, u_a, u_b, gates_a, gates_b, ycat_a, ycat_b, xn_s, ub_s, h_s,
                  acc_s, xbuf, obuf, in_sem, out_sem,
                  *, rows, d_conv, d_lru, hist, n_steps):
    s = pl.program_id(0)
    f32 = jnp.float32
    bf16 = jnp.bfloat16
    n_batch = xbuf.shape[1]
    n_pos = rows // SUBLANES
    blk_pos = 2 * n_pos
    col_b, col_c, col_x, col_g = 0, d_conv, 2 * d_conv, 3 * d_conv
    col_xl, col_gl = 4 * d_conv, 4 * d_conv + d_lru
    lane = lax.broadcasted_iota(jnp.int32, (SUBLANES, LANES), 1)
    low_half = lane < (LANES // 2)

    def x_copy(blk, b):
        slot = blk % X_SLOTS
        return pltpu.make_async_copy(
            x_hbm.at[b, pl.ds(blk * blk_pos, blk_pos), :],
            xbuf.at[pl.ds(slot * blk_pos, blk_pos), b, :],
            in_sem.at[slot])

    def out_copy(blk, b):
        slot = blk % O_SLOTS
        return pltpu.make_async_copy(
            obuf.at[pl.ds(slot * blk_pos, blk_pos), b, :],
            out_hbm.at[b, pl.ds(blk * blk_pos, blk_pos), :],
            out_sem.at[slot])

    def start_all(copy, blk):
        for b in range(n_batch):
            copy(blk, b).start()

    def wait_all(copy, blk):
        for b in range(n_batch):
            copy(blk, b).wait()

    @pl.when(s == 0)
    def _():
        for ref in (proj_b, u_b, gates_b, ycat_a, h_s):
            ref[...] = jnp.zeros(ref.shape, ref.dtype)
        start_all(x_copy, 0)

    @pl.when(s + 1 < n_steps)
    def _():
        start_all(x_copy, s + 1)

    @pl.when(s < n_steps)
    def _():
        wait_all(x_copy, s)

    @pl.when(s >= O_SLOTS + 1)
    def _():
        wait_all(out_copy, s - 1 - O_SLOTS)

    row_zero = pl.multiple_of(jnp.minimum(s, 0) * rows, rows)
    x_in0 = (jnp.minimum(s, n_steps - 1) % X_SLOTS) * blk_pos
    x_res0 = (jnp.maximum(s - 1, 0) % X_SLOTS) * blk_pos
    o_out0 = ((s + 1) % O_SLOTS) * blk_pos

    def stage_in(pos0, proj_w, proj_r, u_w, gates_w):
        per = BF16_ROWS // SUBLANES
        for q in range(n_pos // per):
            xn = [_rms_rows(xbuf[x_in0 + pos0 + p], ln_g_ref[...])
                  for p in range(q * per, (q + 1) * per)]
            xn_s[q * BF16_ROWS:(q + 1) * BF16_ROWS, :] = jnp.concatenate(xn, axis=0).astype(bf16)
        for col, width in ((col_xl, d_lru), (col_c, d_conv), (col_x, d_conv),
                           (col_b, d_conv), (col_g, d_conv), (col_gl, d_lru)):
            for c0 in range(col, col + width, MXU_COLS):
                proj_w[pl.ds(hist, rows), c0:c0 + MXU_COLS] = jnp.dot(
                    xn_s[...], pltpu.bitcast(w_in_ref[:, c0:c0 + MXU_COLS], bf16),
                    preferred_element_type=f32)
        for col, width in ((col_c, d_conv), (col_xl, d_lru)):
            proj_w[0:hist, col:col + width] = proj_r[rows:rows + hist, col:col + width]
        for p in range(n_pos):
            cur = slice(hist + p * SUBLANES, hist + (p + 1) * SUBLANES)
            proj_w[cur, col_c:col_c + d_conv] = (
                proj_w[cur, col_c:col_c + d_conv] * proj_w[cur, col_x:col_x + d_conv])
            u = lconv_b_ref[...]
            for k in range(4):
                back = slice(hist + (p - 3 + k) * SUBLANES, hist + (p - 2 + k) * SUBLANES)
                u = u + proj_w[back, col_xl:col_xl + d_lru] * lconv_w_ref[k]
            u_w[p * SUBLANES:(p + 1) * SUBLANES, :] = u
        for blk in range(rows // NORM_ROWS):
            rs = slice(blk * NORM_ROWS, (blk + 1) * NORM_ROWS)
            ub_s[rs, :] = u_w[rs, :].astype(bf16)
        for g in range(d_lru // GATE_GROUP):
            cols = slice(g * GATE_GROUP, (g + 1) * GATE_GROUP)
            ug = ub_s[:, cols]
            gates_w[:, cols] = jnp.dot(
                ug, pltpu.bitcast(wa_ref[g], bf16), preferred_element_type=f32)
            gates_w[:, d_lru + g * GATE_GROUP:d_lru + (g + 1) * GATE_GROUP] = jnp.dot(
                ug, pltpu.bitcast(wi_ref[g], bf16), preferred_element_type=f32)

    def one_position(p, h, proj_r, u_r, gates_r):
        cur = slice(hist + p * SUBLANES, hist + (p + 1) * SUBLANES)
        rows_p = slice(p * SUBLANES, (p + 1) * SUBLANES)
        conv = proj_r[cur, col_c:col_c + d_conv] * conv_w_ref[2]
        for k in (1, 0):
            back = slice(hist + (p - 2 + k) * SUBLANES, hist + (p - 1 + k) * SUBLANES)
            conv = conv + proj_r[back, col_c:col_c + d_conv] * conv_w_ref[k]
        y = proj_r[cur, col_b:col_b + d_conv] * conv
        gate = _silu_from_half(proj_r[cur, col_g:col_g + d_conv])
        parts = []
        for hd in range(N_CONV_HEADS):
            yh = y[:, hd * LANES:(hd + 1) * LANES]
            ms = jnp.mean(yh * yh, axis=-1, keepdims=True)
            parts.append(yh * lax.rsqrt(ms + RMS_EPS))
        y_conv = (jnp.concatenate(parts, axis=-1) * conv_g_ref[...]) * gate

        hu = u_r[rows_p, :]
        t_r = jnp.tanh(gates_r[rows_p, 0:d_lru] + b_a_ref[...])
        t_i = jnp.tanh(gates_r[rows_p, d_lru:2 * d_lru] + b_i_ref[...])
        log_a = t_r * c_lam_ref[...] + c_lam_ref[...]
        a = jnp.exp(log_a)
        mult = jnp.sqrt(-jnp.tanh(log_a) * (a * a + 1.0))
        h = a * h + mult * (t_i * hu + hu)
        gl = _silu_from_half(proj_r[cur, col_gl:col_gl + d_lru])
        parts = []
        inv_n = 1.0 / (LANES // 2)
        for v in range(d_lru // LANES):
            hv = h[:, v * LANES:(v + 1) * LANES]
            sq = hv * hv
            ss_lo = jnp.sum(jnp.where(low_half, sq, 0.0), axis=-1, keepdims=True)
            ss_hi = jnp.sum(jnp.where(low_half, 0.0, sq), axis=-1, keepdims=True)
            ss = jnp.where(low_half, ss_lo, ss_hi)
            parts.append(hv * lax.rsqrt(ss * inv_n + RMS_EPS))
        y_lru = (jnp.concatenate(parts, axis=-1) * lru_g_ref[...]) * gl
        return y_conv, y_lru, h

    def stage_mix(proj_r, u_r, gates_r, ycat_w, h):
        per = BF16_ROWS // SUBLANES
        for q in range(n_pos // per):
            ycs, yls = [], []
            for p in range(q * per, (q + 1) * per):
                yc, yl, h = one_position(p, h, proj_r, u_r, gates_r)
                ycs.append(yc)
                yls.append(yl)
            out_rows = slice(q * BF16_ROWS, (q + 1) * BF16_ROWS)
            ycat_w[out_rows, 0:d_conv] = jnp.concatenate(ycs, axis=0).astype(bf16)
            ycat_w[out_rows, d_conv:d_conv + d_lru] = jnp.concatenate(yls, axis=0).astype(bf16)
        return h

    def stage_out(ycat_r, pos0):
        d_model = acc_s.shape[1]
        for c0 in range(0, d_model, MXU_COLS):
            acc_s[pl.ds(row_zero, rows), c0:c0 + MXU_COLS] = jnp.dot(
                ycat_r[...], pltpu.bitcast(w_out_ref[:, c0:c0 + MXU_COLS], bf16),
                preferred_element_type=f32)
        for p in range(n_pos):
            hres = xbuf[x_res0 + pos0 + p] + acc_s[p * SUBLANES:(p + 1) * SUBLANES, :]
            obuf[o_out0 + pos0 + p] = _rms_rows(hres, final_g_ref[...])

    def half(pos0, set_w, set_r, ycat_w, ycat_r, h):
        proj_w, u_w, gates_w = set_w
        proj_r, u_r, gates_r = set_r
        h = stage_mix(proj_r, u_r, gates_r, ycat_w, h)
        stage_in(pos0, proj_w, proj_r, u_w, gates_w)
        stage_out(ycat_r, pos0)
        return h

    set_a = (proj_a, u_a, gates_a)
    set_b = (proj_b, u_b, gates_b)
    h = half(0, set_a, set_b, ycat_b, ycat_a, h_s[...])
    h = jnp.where(s == 0, jnp.zeros_like(h), h)
    h = half(n_pos, set_b, set_a, ycat_a, ycat_b, h)
    h_s[...] = h

    @pl.when(s >= 1)
    def _():
        start_all(out_copy, s - 1)

    @pl.when(s == n_steps)
    def _():
        for back in range(O_SLOTS, 0, -1):
            wait_all(out_copy, s - back)


def _block_diag_groups(w, group):
    heads, hd, _ = w.shape
    per = group // hd
    wg = w.reshape(heads // per, per, hd, hd)
    eye = jnp.eye(per, dtype=w.dtype)
    dense = jnp.einsum('gpde,pq->gpdqe', wg, eye)
    return dense.reshape(heads // per, group, group)


def _pack_bf16_rows(w):
    wb = w.astype(jnp.bfloat16)
    *lead, k, n = wb.shape
    pairs = jnp.swapaxes(wb.reshape(*lead, k // 2, 2, n), -1, -2)
    return lax.bitcast_convert_type(pairs, jnp.uint32)


def _rows8(v):
    return jnp.broadcast_to(v.reshape(1, -1), (SUBLANES, v.shape[-1]))


@functools.partial(jax.jit, static_argnames=("chunk_pos",))
def _forward(x, ln_g, w_in, conv_w, lru_conv_w, lru_conv_b, w_a, b_a, w_i, b_i,
             lam, conv_out_g, lru_out_g, w_out, final_g, chunk_pos=32):
    bsz, seq, d_model = x.shape
    d_conv = conv_w.shape[1]
    d_lru = lru_conv_w.shape[1]
    in_cols = w_in.shape[1]
    assert bsz == SUBLANES and in_cols == 4 * d_conv + 2 * d_lru
    assert seq % (2 * chunk_pos) == 0 and d_lru % GATE_GROUP == 0
    rows = chunk_pos * SUBLANES
    hist = (lru_conv_w.shape[0] - 1) * SUBLANES
    n_steps = seq // (2 * chunk_pos)
    assert n_steps > O_SLOTS
    blk_pos = 2 * chunk_pos
    bf16 = jnp.bfloat16
    f32 = jnp.float32

    half_c_lam = (0.5 * RG_LRU_C) * jax.nn.log_sigmoid(lam.astype(f32))
    gate_cols = jnp.concatenate([jnp.ones((3 * d_conv,), f32), jnp.full((d_conv,), 0.5, f32),
                                 jnp.ones((d_lru,), f32), jnp.full((d_lru,), 0.5, f32)])
    w_in_h = w_in * gate_cols[None, :]
    wa_g = _pack_bf16_rows(_block_diag_groups(w_a, GATE_GROUP))
    wi_g = _pack_bf16_rows(_block_diag_groups(w_i, GATE_GROUP))
    conv_w8 = jnp.broadcast_to(conv_w[:, None, :], (conv_w.shape[0], SUBLANES, d_conv))
    lconv_w8 = jnp.broadcast_to(0.5 * lru_conv_w[:, None, :],
                                (lru_conv_w.shape[0], SUBLANES, d_lru))

    def const(shape):
        return pl.BlockSpec(shape, lambda s: (0,) * len(shape), pipeline_mode=pl.Buffered(1))

    kern = functools.partial(_layer_kernel, rows=rows, d_conv=d_conv, d_lru=d_lru, hist=hist,
                             n_steps=n_steps)
    return pl.pallas_call(
        kern,
        grid=(n_steps + 1,),
        in_specs=[
            pl.BlockSpec(memory_space=pl.ANY),
            const((SUBLANES, d_model)),
            const((d_model // 2, in_cols)),
            const(conv_w8.shape),
            const(lconv_w8.shape),
            const((SUBLANES, d_lru)),
            const(wa_g.shape),
            const(wi_g.shape),
            const((SUBLANES, d_lru)),
            const((SUBLANES, d_lru)),
            const((SUBLANES, d_lru)),
            const((SUBLANES, d_conv)),
            const((SUBLANES, d_lru)),
            const(((d_conv + d_lru) // 2, d_model)),
            const((SUBLANES, d_model)),
        ],
        out_specs=pl.BlockSpec(memory_space=pl.ANY),
        out_shape=jax.ShapeDtypeStruct((bsz, seq, d_model), f32),
        scratch_shapes=[
            pltpu.VMEM((rows + hist, in_cols), f32),
            pltpu.VMEM((rows + hist, in_cols), f32),
            pltpu.VMEM((rows, d_lru), f32),
            pltpu.VMEM((rows, d_lru), f32),
            pltpu.VMEM((rows, 2 * d_lru), f32),
            pltpu.VMEM((rows, 2 * d_lru), f32),
            pltpu.VMEM((rows, d_conv + d_lru), bf16),
            pltpu.VMEM((rows, d_conv + d_lru), bf16),
            pltpu.VMEM((rows, d_model), bf16),
            pltpu.VMEM((rows, d_lru), bf16),
            pltpu.VMEM((SUBLANES, d_lru), f32),
            pltpu.VMEM((rows, d_model), f32),
            pltpu.VMEM((X_SLOTS * blk_pos, bsz, d_model), f32),
            pltpu.VMEM((O_SLOTS * blk_pos, bsz, d_model), f32),
            pltpu.SemaphoreType.DMA((X_SLOTS,)),
            pltpu.SemaphoreType.DMA((O_SLOTS,)),
        ],
        compiler_params=pltpu.CompilerParams(
            dimension_semantics=("arbitrary",),
            vmem_limit_bytes=VMEM_LIMIT_BYTES),
        name="hybrid_layer",
    )(x, _rows8(ln_g), _pack_bf16_rows(w_in_h), conv_w8, lconv_w8,
      _rows8(0.5 * lru_conv_b), wa_g, wi_g, _rows8(0.5 * b_a), _rows8(0.5 * b_i),
      _rows8(half_c_lam),
      _rows8(conv_out_g), _rows8(lru_out_g), _pack_bf16_rows(w_out), _rows8(final_g))


def kernel(x, ln_g, w_in, conv_w, lru_conv_w, lru_conv_b, w_a, b_a, w_i, b_i, lam,
           conv_out_g, lru_out_g, w_out, final_g):
    return _forward(x, ln_g, w_in, conv_w, lru_conv_w, lru_conv_b, w_a, b_a, w_i, b_i,
                    lam, conv_out_g, lru_out_g, w_out, final_g)
```

```python
import functools

import jax
import jax.numpy as jnp
from jax import lax
from jax.experimental import pallas as pl
from jax.experimental.pallas import tpu as pltpu

SUBLANES = 8
LANES = 128
BF16_ROWS = 16
RMS_EPS = 1e-6
RG_LRU_C = 8.0
N_CONV_HEADS = 8
N_LRU_HEADS = 16
GATE_GROUP = 256
MXU_COLS = 256
NORM_ROWS = 32
X_SLOTS = 3
O_SLOTS = 2
VMEM_LIMIT_BYTES = 58 * 1024 * 1024


def _silu_from_half(hg):
    return hg + hg * jnp.tanh(hg)


def _rms_rows(v, g_rows):
    ms = jnp.mean(v * v, axis=-1, keepdims=True)
    return (v * lax.rsqrt(ms + RMS_EPS)) * g_rows


def _layer_kernel(x_hbm, ln_g_ref, w_in_ref, conv_w_ref, lconv_w_ref,
                  lconv_b_ref, wa_ref, wi_ref, b_a_ref, b_i_ref, c_lam_ref, conv_g_ref,
                  lru_g_ref, w_out_ref, final_g_ref, out_hbm,
                  proj_a, proj_b, u_a, u_b, gates_a, gates_b, ycat_a, ycat_b, xn_s, ub_s, h_s,
                  acc_s, xbuf, obuf, in_sem, out_sem,
                  *, rows, d_conv, d_lru, hist, n_steps):
    s = pl.program_id(0)
    f32 = jnp.float32
    bf16 = jnp.bfloat16
    n_batch = xbuf.shape[1]
    n_pos = rows // SUBLANES
    blk_pos = 2 * n_pos
    col_b, col_c, col_x, col_g = 0, d_conv, 2 * d_conv, 3 * d_conv
    col_xl, col_gl = 4 * d_conv, 4 * d_conv + d_lru
    lane = lax.broadcasted_iota(jnp.int32, (SUBLANES, LANES), 1)
    low_half = lane < (LANES // 2)

    def x_copy(blk, b):
        slot = blk % X_SLOTS
        return pltpu.make_async_copy(
            x_hbm.at[b, pl.ds(blk * blk_pos, blk_pos), :],
            xbuf.at[pl.ds(slot * blk_pos, blk_pos), b, :],
            in_sem.at[slot])

    def out_copy(blk, b):
        slot = blk % O_SLOTS
        return pltpu.make_async_copy(
            obuf.at[pl.ds(slot * blk_pos, blk_pos), b, :],
            out_hbm.at[b, pl.ds(blk * blk_pos, blk_pos), :],
            out_sem.at[slot])

    def start_all(copy, blk):
        for b in range(n_batch):
            copy(blk, b).start()

    def wait_all(copy, blk):
        for b in range(n_batch):
            copy(blk, b).wait()

    @pl.when(s == 0)
    def _():
        for ref in (proj_b, u_b, gates_b, ycat_a, h_s):
            ref[...] = jnp.zeros(ref.shape, ref.dtype)
        start_all(x_copy, 0)

    @pl.when(s + 1 < n_steps)
    def _():
        start_all(x_copy, s + 1)

    @pl.when(s < n_steps)
    def _():
        wait_all(x_copy, s)

    @pl.when(s >= O_SLOTS + 1)
    def _():
        wait_all(out_copy, s - 1 - O_SLOTS)

    row_zero = pl.multiple_of(jnp.minimum(s, 0) * rows, rows)
    x_in0 = (jnp.minimum(s, n_steps - 1) % X_SLOTS) * blk_pos
    x_res0 = (jnp.maximum(s - 1, 0) % X_SLOTS) * blk_pos
    o_out0 = ((s + 1) % O_SLOTS) * blk_pos

    def stage_in(pos0, proj_w, proj_r, u_w, gates_w):
        per = BF16_ROWS // SUBLANES
        for q in range(n_pos // per):
            xn = [_rms_rows(xbuf[x_in0 + pos0 + p], ln_g_ref[...])
                  for p in range(q * per, (q + 1) * per)]
            xn_s[q * BF16_ROWS:(q + 1) * BF16_ROWS, :] = jnp.concatenate(xn, axis=0).astype(bf16)
        for col, width in ((col_xl, d_lru), (col_c, d_conv), (col_x, d_conv),
                           (col_b, d_conv), (col_g, d_conv), (col_gl, d_lru)):
            for c0 in range(col, col + width, MXU_COLS):
                proj_w[pl.ds(hist, rows), c0:c0 + MXU_COLS] = jnp.dot(
                    xn_s[...], pltpu.bitcast(w_in_ref[:, c0:c0 + MXU_COLS], bf16),
                    preferred_element_type=f32)
        for col, width in ((col_c, d_conv), (col_xl, d_lru)):
            proj_w[0:hist, col:col + width] = proj_r[rows:rows + hist, col:col + width]
        for p in range(n_pos):
            cur = slice(hist + p * SUBLANES, hist + (p + 1) * SUBLANES)
            proj_w[cur, col_c:col_c + d_conv] = (
                proj_w[cur, col_c:col_c + d_conv] * proj_w[cur, col_x:col_x + d_conv])
            u = lconv_b_ref[...]
            for k in range(4):
                back = slice(hist + (p - 3 + k) * SUBLANES, hist + (p - 2 + k) * SUBLANES)
                u = u + proj_w[back, col_xl:col_xl + d_lru] * lconv_w_ref[k]
            u_w[p * SUBLANES:(p + 1) * SUBLANES, :] = u
        for blk in range(rows // NORM_ROWS):
            rs = slice(blk * NORM_ROWS, (blk + 1) * NORM_ROWS)
            ub_s[rs, :] = u_w[rs, :].astype(bf16)
        for g in range(d_lru // GATE_GROUP):
            cols = slice(g * GATE_GROUP, (g + 1) * GATE_GROUP)
            ug = ub_s[:, cols]
            gates_w[:, cols] = jnp.dot(
                ug, pltpu.bitcast(wa_ref[g], bf16), preferred_element_type=f32)
            gates_w[:, d_lru + g * GATE_GROUP:d_lru + (g + 1) * GATE_GROUP] = jnp.dot(
                ug, pltpu.bitcast(wi_ref[g], bf16), preferred_element_type=f32)

    def one_position(p, h, proj_r, u_r, gates_r):
        cur = slice(hist + p * SUBLANES, hist + (p + 1) * SUBLANES)
        rows_p = slice(p * SUBLANES, (p + 1) * SUBLANES)
        conv = proj_r[cur, col_c:col_c + d_conv] * conv_w_ref[2]
        for k in (1, 0):
            back = slice(hist + (p - 2 + k) * SUBLANES, hist + (p - 1 + k) * SUBLANES)
            conv = conv + proj_r[back, col_c:col_c + d_conv] * conv_w_ref[k]
        y = proj_r[cur, col_b:col_b + d_conv] * conv
        gate = _silu_from_half(proj_r[cur, col_g:col_g + d_conv])
        parts = []
        for hd in range(N_CONV_HEADS):
            yh = y[:, hd * LANES:(hd + 1) * LANES]
            ms = jnp.mean(yh * yh, axis=-1, keepdims=True)
            parts.append(yh * lax.rsqrt(ms + RMS_EPS))
        y_conv = (jnp.concatenate(parts, axis=-1) * conv_g_ref[...]) * gate

        hu = u_r[rows_p, :]
        t_r = jnp.tanh(gates_r[rows_p, 0:d_lru] + b_a_ref[...])
        t_i = jnp.tanh(gates_r[rows_p, d_lru:2 * d_lru] + b_i_ref[...])
        log_a = t_r * c_lam_ref[...] + c_lam_ref[...]
        a = jnp.exp(log_a)
        mult = jnp.sqrt(-jnp.tanh(log_a) * (a * a + 1.0))
        h = a * h + mult * (t_i * hu + hu)
        gl = _silu_from_half(proj_r[cur, col_gl:col_gl + d_lru])
        parts = []
        inv_n = 1.0 / (LANES // 2)
        for v in range(d_lru // LANES):
            hv = h[:, v * LANES:(v + 1) * LANES]
            sq = hv * hv
            ss_lo = jnp.sum(jnp.where(low_half, sq, 0.0), axis=-1, keepdims=True)
            ss_hi = jnp.sum(jnp.where(low_half, 0.0, sq), axis=-1, keepdims=True)
            ss = jnp.where(low_half, ss_lo, ss_hi)
            parts.append(hv * lax.rsqrt(ss * inv_n + RMS_EPS))
        y_lru = (jnp.concatenate(parts, axis=-1) * lru_g_ref[...]) * gl
        return y_conv, y_lru, h

    def stage_mix(proj_r, u_r, gates_r, ycat_w, h):
        per = BF16_ROWS // SUBLANES
        for q in range(n_pos // per):
            ycs, yls = [], []
            for p in range(q * per, (q + 1) * per):
                yc, yl, h = one_position(p, h, proj_r, u_r, gates_r)
                ycs.append(yc)
                yls.append(yl)
            out_rows = slice(q * BF16_ROWS, (q + 1) * BF16_ROWS)
            ycat_w[out_rows, 0:d_conv] = jnp.concatenate(ycs, axis=0).astype(bf16)
            ycat_w[out_rows, d_conv:d_conv + d_lru] = jnp.concatenate(yls, axis=0).astype(bf16)
        return h

    def stage_out(ycat_r, pos0):
        d_model = acc_s.shape[1]
        for c0 in range(0, d_model, MXU_COLS):
            acc_s[pl.ds(row_zero, rows), c0:c0 + MXU_COLS] = jnp.dot(
                ycat_r[...], pltpu.bitcast(w_out_ref[:, c0:c0 + MXU_COLS], bf16),
                preferred_element_type=f32)
        for p in range(n_pos):
            hres = xbuf[x_res0 + pos0 + p] + acc_s[p * SUBLANES:(p + 1) * SUBLANES, :]
            obuf[o_out0 + pos0 + p] = _rms_rows(hres, final_g_ref[...])

    def half(pos0, set_w, set_r, ycat_w, ycat_r, h):
        proj_w, u_w, gates_w = set_w
        proj_r, u_r, gates_r = set_r
        h = stage_mix(proj_r, u_r, gates_r, ycat_w, h)
        stage_in(pos0, proj_w, proj_r, u_w, gates_w)
        stage_out(ycat_r, pos0)
        return h

    set_a = (proj_a, u_a, gates_a)
    set_b = (proj_b, u_b, gates_b)
    h = half(0, set_a, set_b, ycat_b, ycat_a, h_s[...])
    h = jnp.where(s == 0, jnp.zeros_like(h), h)
    h = half(n_pos, set_b, set_a, ycat_a, ycat_b, h)
    h_s[...] = h

    @pl.when(s >= 1)
    def _():
        start_all(out_copy, s - 1)

    @pl.when(s == n_steps)
    def _():
        for back in range(O_SLOTS, 0, -1):
            wait_all(out_copy, s - back)


def _block_diag_groups(w, group):
    heads, hd, _ = w.shape
    per = group // hd
    wg = w.reshape(heads // per, per, hd, hd)
    eye = jnp.eye(per, dtype=w.dtype)
    dense = jnp.einsum('gpde,pq->gpdqe', wg, eye)
    return dense.reshape(heads // per, group, group)


def _pack_bf16_rows(w):
    bits = lax.bitcast_convert_type(w.astype(jnp.bfloat16), jnp.uint16).astype(jnp.uint32)
    return bits[..., 0::2, :] | (bits[..., 1::2, :] << 16)


def _rows8(v):
    return jnp.broadcast_to(v.reshape(1, -1), (SUBLANES, v.shape[-1]))


@functools.partial(jax.jit, static_argnames=("chunk_pos",))
def _forward(x, ln_g, w_in, conv_w, lru_conv_w, lru_conv_b, w_a, b_a, w_i, b_i,
             lam, conv_out_g, lru_out_g, w_out, final_g, chunk_pos=32):
    bsz, seq, d_model = x.shape
    d_conv = conv_w.shape[1]
    d_lru = lru_conv_w.shape[1]
    in_cols = w_in.shape[1]
    assert bsz == SUBLANES and in_cols == 4 * d_conv + 2 * d_lru
    assert seq % (2 * chunk_pos) == 0 and d_lru % GATE_GROUP == 0
    rows = chunk_pos * SUBLANES
    hist = (lru_conv_w.shape[0] - 1) * SUBLANES
    n_steps = seq // (2 * chunk_pos)
    assert n_steps > O_SLOTS
    blk_pos = 2 * chunk_pos
    bf16 = jnp.bfloat16
    f32 = jnp.float32

    half_c_lam = (0.5 * RG_LRU_C) * jax.nn.log_sigmoid(lam.astype(f32))
    gate_cols = jnp.concatenate([jnp.ones((3 * d_conv,), f32), jnp.full((d_conv,), 0.5, f32),
                                 jnp.ones((d_lru,), f32), jnp.full((d_lru,), 0.5, f32)])
    w_in_h = w_in * gate_cols[None, :]
    wa_g = _pack_bf16_rows(_block_diag_groups(w_a, GATE_GROUP))
    wi_g = _pack_bf16_rows(_block_diag_groups(w_i, GATE_GROUP))
    conv_w8 = jnp.broadcast_to(conv_w[:, None, :], (conv_w.shape[0], SUBLANES, d_conv))
    lconv_w8 = jnp.broadcast_to(0.5 * lru_conv_w[:, None, :],
                                (lru_conv_w.shape[0], SUBLANES, d_lru))

    def const(shape):
        return pl.BlockSpec(shape, lambda s: (0,) * len(shape), pipeline_mode=pl.Buffered(1))

    kern = functools.partial(_layer_kernel, rows=rows, d_conv=d_conv, d_lru=d_lru, hist=hist,
                             n_steps=n_steps)
    return pl.pallas_call(
        kern,
        grid=(n_steps + 1,),
        in_specs=[
            pl.BlockSpec(memory_space=pl.ANY),
            const((SUBLANES, d_model)),
            const((d_model // 2, in_cols)),
            const(conv_w8.shape),
            const(lconv_w8.shape),
            const((SUBLANES, d_lru)),
            const(wa_g.shape),
            const(wi_g.shape),
            const((SUBLANES, d_lru)),
            const((SUBLANES, d_lru)),
            const((SUBLANES, d_lru)),
            const((SUBLANES, d_conv)),
            const((SUBLANES, d_lru)),
            const(((d_conv + d_lru) // 2, d_model)),
            const((SUBLANES, d_model)),
        ],
        out_specs=pl.BlockSpec(memory_space=pl.ANY),
        out_shape=jax.ShapeDtypeStruct((bsz, seq, d_model), f32),
        scratch_shapes=[
            pltpu.VMEM((rows + hist, in_cols), f32),
            pltpu.VMEM((rows + hist, in_cols), f32),
            pltpu.VMEM((rows, d_lru), f32),
            pltpu.VMEM((rows, d_lru), f32),
            pltpu.VMEM((rows, 2 * d_lru), f32),
            pltpu.VMEM((rows, 2 * d_lru), f32),
            pltpu.VMEM((rows, d_conv + d_lru), bf16),
            pltpu.VMEM((rows, d_conv + d_lru), bf16),
            pltpu.VMEM((rows, d_model), bf16),
            pltpu.VMEM((rows, d_lru), bf16),
            pltpu.VMEM((SUBLANES, d_lru), f32),
            pltpu.VMEM((rows, d_model), f32),
            pltpu.VMEM((X_SLOTS * blk_pos, bsz, d_model), f32),
            pltpu.VMEM((O_SLOTS * blk_pos, bsz, d_model), f32),
            pltpu.SemaphoreType.DMA((X_SLOTS,)),
            pltpu.SemaphoreType.DMA((O_SLOTS,)),
        ],
        compiler_params=pltpu.CompilerParams(
            dimension_semantics=("arbitrary",),
            vmem_limit_bytes=VMEM_LIMIT_BYTES),
        name="hybrid_layer",
    )(x, _rows8(ln_g), _pack_bf16_rows(w_in_h), conv_w8, lconv_w8,
      _rows8(0.5 * lru_conv_b), wa_g, wi_g, _rows8(0.5 * b_a), _rows8(0.5 * b_i),
      _rows8(half_c_lam),
      _rows8(conv_out_g), _rows8(lru_out_g), _pack_bf16_rows(w_out), _rows8(final_g))


def kernel(x, ln_g, w_in, conv_w, lru_conv_w, lru_conv_b, w_a, b_a, w_i, b_i, lam,
           conv_out_g, lru_out_g, w_out, final_g):
    return _forward(x, ln_g, w_in, conv_w, lru_conv_w, lru_conv_b, w_a, b_a, w_i, b_i,
                    lam, conv_out_g, lru_out_g, w_out, final_g)
```

```python
import functools

import jax
import jax.numpy as jnp
from jax import lax
from jax.experimental import pallas as pl
from jax.experimental.pallas import tpu as pltpu

SUBLANES = 8
LANES = 128
BF16_ROWS = 16
RMS_EPS = 1e-6
RG_LRU_C = 8.0
N_CONV_HEADS = 8
N_LRU_HEADS = 16
GATE_GROUP = 256
MXU_COLS = 256
NORM_ROWS = 32
W_STAGE_ROWS = 256
(PV_LN_G, PV_FINAL_G, PV_CONV_G, PV_LRU_G, PV_B_A, PV_B_I, PV_C_LAM, PV_LCONV_B,
 PV_CONV_W, PV_LCONV_W) = (0, 1, 2, 3, 4, 5, 6, 7, 8, 11)
X_SLOTS = 3
O_SLOTS = 2
VMEM_LIMIT_BYTES = 58 * 1024 * 1024


def _silu_from_half(hg):
    return hg + hg * jnp.tanh(hg)


def _rms_rows(v, g_rows):
    ms = jnp.mean(v * v, axis=-1, keepdims=True)
    return (v * lax.rsqrt(ms + RMS_EPS)) * g_rows


def _layer_kernel(x_hbm, w_in_hbm, w_out_hbm, wa_ref, wi_ref, pv_ref, out_hbm,
                  proj_a, proj_b, u_a, u_b, gates_a, gates_b, ycat_a, ycat_b, xn_s, ub_s, h_s,
                  acc_s, xbuf, obuf, w_in_s, w_out_s, wa_s, wi_s, in_sem, out_sem, w_sem,
                  *, rows, d_conv, d_lru, hist, n_steps):
    s = pl.program_id(0)
    f32 = jnp.float32
    bf16 = jnp.bfloat16
    n_batch = xbuf.shape[1]
    n_pos = rows // SUBLANES
    blk_pos = 2 * n_pos
    col_b, col_c, col_x, col_g = 0, d_conv, 2 * d_conv, 3 * d_conv
    col_xl, col_gl = 4 * d_conv, 4 * d_conv + d_lru
    lane = lax.broadcasted_iota(jnp.int32, (SUBLANES, LANES), 1)
    low_half = lane < (LANES // 2)

    def pv(i):
        return pv_ref[i]

    def x_copy(blk, b):
        slot = blk % X_SLOTS
        return pltpu.make_async_copy(
            x_hbm.at[b, pl.ds(blk * blk_pos, blk_pos), :],
            xbuf.at[pl.ds(slot * blk_pos, blk_pos), b, :],
            in_sem.at[slot])

    def out_copy(blk, b):
        slot = blk % O_SLOTS
        return pltpu.make_async_copy(
            obuf.at[pl.ds(slot * blk_pos, blk_pos), b, :],
            out_hbm.at[b, pl.ds(blk * blk_pos, blk_pos), :],
            out_sem.at[slot])

    def start_all(copy, blk):
        for b in range(n_batch):
            copy(blk, b).start()

    def wait_all(copy, blk):
        for b in range(n_batch):
            copy(blk, b).wait()

    def load_weights():
        stages = (proj_a, proj_b)
        jobs = []
        for r0 in range(0, w_in_hbm.shape[0], W_STAGE_ROWS):
            jobs.append((w_in_hbm, w_in_s, r0))
        for r0 in range(0, w_out_hbm.shape[0], W_STAGE_ROWS):
            jobs.append((w_out_hbm, w_out_s, r0))

        def job_copy(j):
            src, _, r0 = jobs[j]
            return pltpu.make_async_copy(
                src.at[pl.ds(r0, W_STAGE_ROWS), :],
                stages[j % 2].at[pl.ds(0, W_STAGE_ROWS), pl.ds(0, src.shape[1])],
                w_sem.at[j % 2])

        job_copy(0).start()
        for j, (src, dst, r0) in enumerate(jobs):
            if j + 1 < len(jobs):
                job_copy(j + 1).start()
            job_copy(j).wait()
            stage = stages[j % 2]
            halved = (col_g, col_gl) if src is w_in_hbm else ()

            def convert(i, carry, stage=stage, dst=dst, r0=r0, halved=halved, width=src.shape[1]):
                r = pl.multiple_of(i * BF16_ROWS, BF16_ROWS)
                for c0 in range(0, width, d_conv):
                    v = stage[pl.ds(r, BF16_ROWS), c0:c0 + d_conv]
                    if c0 in halved:
                        v = v * 0.5
                    dst[pl.ds(r0 + r, BF16_ROWS), c0:c0 + d_conv] = v.astype(bf16)
                return carry

            lax.fori_loop(0, W_STAGE_ROWS // BF16_ROWS, convert, 0)
        wa_s[...] = wa_ref[...]
        wi_s[...] = wi_ref[...]

    @pl.when(s == 0)
    def _():
        start_all(x_copy, 0)
        load_weights()
        for ref in (proj_b, u_b, gates_b, ycat_a, h_s):
            ref[...] = jnp.zeros(ref.shape, ref.dtype)

    @pl.when(s + 1 < n_steps)
    def _():
        start_all(x_copy, s + 1)

    @pl.when(s < n_steps)
    def _():
        wait_all(x_copy, s)

    @pl.when(s >= O_SLOTS + 1)
    def _():
        wait_all(out_copy, s - 1 - O_SLOTS)

    row_zero = pl.multiple_of(jnp.minimum(s, 0) * rows, rows)
    x_in0 = (jnp.minimum(s, n_steps - 1) % X_SLOTS) * blk_pos
    x_res0 = (jnp.maximum(s - 1, 0) % X_SLOTS) * blk_pos
    o_out0 = ((s + 1) % O_SLOTS) * blk_pos

    def stage_in(pos0, proj_w, proj_r, u_w, gates_w):
        per = BF16_ROWS // SUBLANES
        for q in range(n_pos // per):
            xn = [_rms_rows(xbuf[x_in0 + pos0 + p], pv(PV_LN_G))
                  for p in range(q * per, (q + 1) * per)]
            xn_s[q * BF16_ROWS:(q + 1) * BF16_ROWS, :] = jnp.concatenate(xn, axis=0).astype(bf16)
        for col, width in ((col_xl, d_lru), (col_c, d_conv), (col_x, d_conv),
                           (col_b, d_conv), (col_g, d_conv), (col_gl, d_lru)):
            for c0 in range(col, col + width, MXU_COLS):
                proj_w[pl.ds(hist, rows), c0:c0 + MXU_COLS] = jnp.dot(
                    xn_s[...], w_in_s[:, c0:c0 + MXU_COLS],
                    preferred_element_type=f32)
        for col, width in ((col_c, d_conv), (col_xl, d_lru)):
            proj_w[0:hist, col:col + width] = proj_r[rows:rows + hist, col:col + width]
        for p in range(n_pos):
            cur = slice(hist + p * SUBLANES, hist + (p + 1) * SUBLANES)
            proj_w[cur, col_c:col_c + d_conv] = (
                proj_w[cur, col_c:col_c + d_conv] * proj_w[cur, col_x:col_x + d_conv])
            u = pv(PV_LCONV_B)
            for k in range(4):
                back = slice(hist + (p - 3 + k) * SUBLANES, hist + (p - 2 + k) * SUBLANES)
                u = u + proj_w[back, col_xl:col_xl + d_lru] * pv(PV_LCONV_W + k)
            u_w[p * SUBLANES:(p + 1) * SUBLANES, :] = u
        for blk in range(rows // NORM_ROWS):
            rs = slice(blk * NORM_ROWS, (blk + 1) * NORM_ROWS)
            ub_s[rs, :] = u_w[rs, :].astype(bf16)
        for g in range(d_lru // GATE_GROUP):
            cols = slice(g * GATE_GROUP, (g + 1) * GATE_GROUP)
            ug = ub_s[:, cols]
            gates_w[:, cols] = jnp.dot(ug, wa_s[g], preferred_element_type=f32)
            gates_w[:, d_lru + g * GATE_GROUP:d_lru + (g + 1) * GATE_GROUP] = jnp.dot(
                ug, wi_s[g], preferred_element_type=f32)

    def one_position(p, h, proj_r, u_r, gates_r):
        cur = slice(hist + p * SUBLANES, hist + (p + 1) * SUBLANES)
        rows_p = slice(p * SUBLANES, (p + 1) * SUBLANES)
        conv = proj_r[cur, col_c:col_c + d_conv] * pv(PV_CONV_W + 2)
        for k in (1, 0):
            back = slice(hist + (p - 2 + k) * SUBLANES, hist + (p - 1 + k) * SUBLANES)
            conv = conv + proj_r[back, col_c:col_c + d_conv] * pv(PV_CONV_W + k)
        y = proj_r[cur, col_b:col_b + d_conv] * conv
        gate = _silu_from_half(proj_r[cur, col_g:col_g + d_conv])
        parts = []
        for hd in range(N_CONV_HEADS):
            yh = y[:, hd * LANES:(hd + 1) * LANES]
            ms = jnp.mean(yh * yh, axis=-1, keepdims=True)
            parts.append(yh * lax.rsqrt(ms + RMS_EPS))
        y_conv = (jnp.concatenate(parts, axis=-1) * pv(PV_CONV_G)) * gate

        hu = u_r[rows_p, :]
        t_r = jnp.tanh(gates_r[rows_p, 0:d_lru] + pv(PV_B_A))
        t_i = jnp.tanh(gates_r[rows_p, d_lru:2 * d_lru] + pv(PV_B_I))
        log_a = t_r * pv(PV_C_LAM) + pv(PV_C_LAM)
        a = jnp.exp(log_a)
        mult = jnp.sqrt(-jnp.tanh(log_a) * (a * a + 1.0))
        h = a * h + mult * (t_i * hu + hu)
        gl = _silu_from_half(proj_r[cur, col_gl:col_gl + d_lru])
        parts = []
        inv_n = 1.0 / (LANES // 2)
        for v in range(d_lru // LANES):
            hv = h[:, v * LANES:(v + 1) * LANES]
            sq = hv * hv
            ss_lo = jnp.sum(jnp.where(low_half, sq, 0.0), axis=-1, keepdims=True)
            ss_hi = jnp.sum(jnp.where(low_half, 0.0, sq), axis=-1, keepdims=True)
            ss = jnp.where(low_half, ss_lo, ss_hi)
            parts.append(hv * lax.rsqrt(ss * inv_n + RMS_EPS))
        y_lru = (jnp.concatenate(parts, axis=-1) * pv(PV_LRU_G)) * gl
        return y_conv, y_lru, h

    def stage_mix(proj_r, u_r, gates_r, ycat_w, h):
        per = BF16_ROWS // SUBLANES
        for q in range(n_pos // per):
            ycs, yls = [], []
            for p in range(q * per, (q + 1) * per):
                yc, yl, h = one_position(p, h, proj_r, u_r, gates_r)
                ycs.append(yc)
                yls.append(yl)
            out_rows = slice(q * BF16_ROWS, (q + 1) * BF16_ROWS)
            ycat_w[out_rows, 0:d_conv] = jnp.concatenate(ycs, axis=0).astype(bf16)
            ycat_w[out_rows, d_conv:d_conv + d_lru] = jnp.concatenate(yls, axis=0).astype(bf16)
        return h

    def stage_out(ycat_r, pos0):
        d_model = acc_s.shape[1]
        for c0 in range(0, d_model, MXU_COLS):
            acc_s[pl.ds(row_zero, rows), c0:c0 + MXU_COLS] = jnp.dot(
                ycat_r[...], w_out_s[:, c0:c0 + MXU_COLS], preferred_element_type=f32)
        for p in range(n_pos):
            hres = xbuf[x_res0 + pos0 + p] + acc_s[p * SUBLANES:(p + 1) * SUBLANES, :]
            obuf[o_out0 + pos0 + p] = _rms_rows(hres, pv(PV_FINAL_G))

    def half(pos0, set_w, set_r, ycat_w, ycat_r, h):
        proj_w, u_w, gates_w = set_w
        proj_r, u_r, gates_r = set_r
        h = stage_mix(proj_r, u_r, gates_r, ycat_w, h)
        stage_in(pos0, proj_w, proj_r, u_w, gates_w)
        stage_out(ycat_r, pos0)
        return h

    set_a = (proj_a, u_a, gates_a)
    set_b = (proj_b, u_b, gates_b)
    h = half(0, set_a, set_b, ycat_b, ycat_a, h_s[...])
    h = jnp.where(s == 0, jnp.zeros_like(h), h)
    h = half(n_pos, set_b, set_a, ycat_a, ycat_b, h)
    h_s[...] = h

    @pl.when(s >= 1)
    def _():
        start_all(out_copy, s - 1)

    @pl.when(s == n_steps)
    def _():
        for back in range(O_SLOTS, 0, -1):
            wait_all(out_copy, s - back)


def _block_diag_groups(w, group):
    heads, hd, _ = w.shape
    per = group // hd
    wg = w.reshape(heads // per, per, hd, hd)
    eye = jnp.eye(per, dtype=w.dtype)
    dense = jnp.einsum('gpde,pq->gpdqe', wg, eye)
    return dense.reshape(heads // per, group, group)


@functools.partial(jax.jit, static_argnames=("chunk_pos",))
def _forward(x, ln_g, w_in, conv_w, lru_conv_w, lru_conv_b, w_a, b_a, w_i, b_i,
             lam, conv_out_g, lru_out_g, w_out, final_g, chunk_pos=32):
    bsz, seq, d_model = x.shape
    d_conv = conv_w.shape[1]
    d_lru = lru_conv_w.shape[1]
    in_cols = w_in.shape[1]
    assert bsz == SUBLANES and in_cols == 4 * d_conv + 2 * d_lru
    assert d_conv == d_lru == d_model
    assert seq % (2 * chunk_pos) == 0 and d_lru % GATE_GROUP == 0
    assert w_in.shape[0] % W_STAGE_ROWS == 0 and w_out.shape[0] % W_STAGE_ROWS == 0
    rows = chunk_pos * SUBLANES
    assert rows >= W_STAGE_ROWS
    hist = (lru_conv_w.shape[0] - 1) * SUBLANES
    n_steps = seq // (2 * chunk_pos)
    assert n_steps > O_SLOTS
    blk_pos = 2 * chunk_pos
    bf16 = jnp.bfloat16
    f32 = jnp.float32

    half_c_lam = (0.5 * RG_LRU_C) * jax.nn.log_sigmoid(lam.astype(f32))
    table = jnp.stack([ln_g, final_g, conv_out_g, lru_out_g, 0.5 * b_a, 0.5 * b_i, half_c_lam,
                       0.5 * lru_conv_b, *conv_w, *(0.5 * lru_conv_w)]).astype(f32)
    assert table.shape[0] == PV_LCONV_W + lru_conv_w.shape[0] == PV_CONV_W + conv_w.shape[0] + 4
    table8 = jnp.broadcast_to(table[:, None, :], (table.shape[0], SUBLANES, d_model))
    wa_g = _block_diag_groups(w_a, GATE_GROUP).astype(bf16)
    wi_g = _block_diag_groups(w_i, GATE_GROUP).astype(bf16)

    def const(shape):
        return pl.BlockSpec(shape, lambda s: (0,) * len(shape), pipeline_mode=pl.Buffered(1))

    kern = functools.partial(_layer_kernel, rows=rows, d_conv=d_conv, d_lru=d_lru, hist=hist,
                             n_steps=n_steps)
    return pl.pallas_call(
        kern,
        grid=(n_steps + 1,),
        in_specs=[
            pl.BlockSpec(memory_space=pl.ANY),
            pl.BlockSpec(memory_space=pl.ANY),
            pl.BlockSpec(memory_space=pl.ANY),
            const(wa_g.shape),
            const(wi_g.shape),
            const(table8.shape),
        ],
        out_specs=pl.BlockSpec(memory_space=pl.ANY),
        out_shape=jax.ShapeDtypeStruct((bsz, seq, d_model), f32),
        scratch_shapes=[
            pltpu.VMEM((rows + hist, in_cols), f32),
            pltpu.VMEM((rows + hist, in_cols), f32),
            pltpu.VMEM((rows, d_lru), f32),
            pltpu.VMEM((rows, d_lru), f32),
            pltpu.VMEM((rows, 2 * d_lru), f32),
            pltpu.VMEM((rows, 2 * d_lru), f32),
            pltpu.VMEM((rows, d_conv + d_lru), bf16),
            pltpu.VMEM((rows, d_conv + d_lru), bf16),
            pltpu.VMEM((rows, d_model), bf16),
            pltpu.VMEM((rows, d_lru), bf16),
            pltpu.VMEM((SUBLANES, d_lru), f32),
            pltpu.VMEM((rows, d_model), f32),
            pltpu.VMEM((X_SLOTS * blk_pos, bsz, d_model), f32),
            pltpu.VMEM((O_SLOTS * blk_pos, bsz, d_model), f32),
            pltpu.VMEM(w_in.shape, bf16),
            pltpu.VMEM(w_out.shape, bf16),
            pltpu.VMEM(wa_g.shape, bf16),
            pltpu.VMEM(wi_g.shape, bf16),
            pltpu.SemaphoreType.DMA((X_SLOTS,)),
            pltpu.SemaphoreType.DMA((O_SLOTS,)),
            pltpu.SemaphoreType.DMA((2,)),
        ],
        compiler_params=pltpu.CompilerParams(
            dimension_semantics=("arbitrary",),
            vmem_limit_bytes=VMEM_LIMIT_BYTES),
        name="hybrid_layer",
    )(x, w_in, w_out, wa_g, wi_g, table8)


def kernel(x, ln_g, w_in, conv_w, lru_conv_w, lru_conv_b, w_a, b_a, w_i, b_i, lam,
           conv_out_g, lru_out_g, w_out, final_g):
    return _forward(x, ln_g, w_in, conv_w, lru_conv_w, lru_conv_b, w_a, b_a, w_i, b_i,
                    lam, conv_out_g, lru_out_g, w_out, final_g)
```

```python
import functools

import jax
import jax.numpy as jnp
from jax import lax
from jax.experimental import pallas as pl
from jax.experimental.pallas import tpu as pltpu

SUBLANES = 8
LANES = 128
BF16_ROWS = 16
RMS_EPS = 1e-6
RG_LRU_C = 8.0
N_CONV_HEADS = 8
N_LRU_HEADS = 16
GATE_GROUP = 256
MXU_COLS = 256
NORM_ROWS = 32
W_STAGE_ROWS = 128
W_STAGES_PER_BUF = 2
(PV_LN_G, PV_FINAL_G, PV_CONV_G, PV_LRU_G, PV_B_A, PV_B_I, PV_C_LAM, PV_LCONV_B,
 PV_CONV_W, PV_LCONV_W) = (0, 1, 2, 3, 4, 5, 6, 7, 8, 11)
X_SLOTS = 3
O_SLOTS = 2
VMEM_LIMIT_BYTES = 58 * 1024 * 1024


def _silu_from_half(hg):
    return hg + hg * jnp.tanh(hg)


def _rms_rows(v, g_rows):
    ms = jnp.mean(v * v, axis=-1, keepdims=True)
    return (v * lax.rsqrt(ms + RMS_EPS)) * g_rows


def _layer_kernel(x_hbm, w_in_hbm, w_out_hbm, wa_ref, wi_ref, pv_ref, out_hbm,
                  proj_a, proj_b, u_a, u_b, gates_a, gates_b, ycat_a, ycat_b, xn_s, ub_s, h_s,
                  acc_s, xbuf, obuf, w_in_s, w_out_s, wa_s, wi_s, in_sem, out_sem, w_sem,
                  *, rows, d_conv, d_lru, hist, n_steps):
    s = pl.program_id(0)
    f32 = jnp.float32
    bf16 = jnp.bfloat16
    n_batch = xbuf.shape[1]
    n_pos = rows // SUBLANES
    blk_pos = 2 * n_pos
    col_b, col_c, col_x, col_g = 0, d_conv, 2 * d_conv, 3 * d_conv
    col_xl, col_gl = 4 * d_conv, 4 * d_conv + d_lru
    lane = lax.broadcasted_iota(jnp.int32, (SUBLANES, LANES), 1)
    low_half = lane < (LANES // 2)

    def pv(i):
        return pv_ref[i]

    def x_copy(blk, b):
        slot = blk % X_SLOTS
        return pltpu.make_async_copy(
            x_hbm.at[b, pl.ds(blk * blk_pos, blk_pos), :],
            xbuf.at[pl.ds(slot * blk_pos, blk_pos), b, :],
            in_sem.at[slot])

    def out_copy(blk, b):
        slot = blk % O_SLOTS
        return pltpu.make_async_copy(
            obuf.at[pl.ds(slot * blk_pos, blk_pos), b, :],
            out_hbm.at[b, pl.ds(blk * blk_pos, blk_pos), :],
            out_sem.at[slot])

    def start_all(copy, blk):
        for b in range(n_batch):
            copy(blk, b).start()

    def wait_all(copy, blk):
        for b in range(n_batch):
            copy(blk, b).wait()

    def load_weights():
        stages = [(buf, r0) for buf in (proj_a, proj_b)
                  for r0 in range(0, W_STAGES_PER_BUF * W_STAGE_ROWS, W_STAGE_ROWS)]
        jobs = [(src, dst, r0) for src, dst in ((w_in_hbm, w_in_s), (w_out_hbm, w_out_s))
                for r0 in range(0, src.shape[0], W_STAGE_ROWS)]

        def stage_view(j, width):
            buf, r0 = stages[j % len(stages)]
            return buf.at[pl.ds(r0, W_STAGE_ROWS), pl.ds(0, width)]

        def job_copy(j):
            src, _, r0 = jobs[j]
            return pltpu.make_async_copy(src.at[pl.ds(r0, W_STAGE_ROWS), :],
                                         stage_view(j, src.shape[1]),
                                         w_sem.at[j % len(stages)])

        ahead = len(stages) - 1
        for j in range(min(ahead, len(jobs))):
            job_copy(j).start()
        for j, (src, dst, r0) in enumerate(jobs):
            if j + ahead < len(jobs):
                job_copy(j + ahead).start()
            job_copy(j).wait()
            stage = stage_view(j, src.shape[1])
            halved = (col_g, col_gl) if src is w_in_hbm else ()

            def convert(i, carry, stage=stage, dst=dst, r0=r0, halved=halved, width=src.shape[1]):
                r = pl.multiple_of(i * BF16_ROWS, BF16_ROWS)
                for c0 in range(0, width, d_conv):
                    v = stage[pl.ds(r, BF16_ROWS), c0:c0 + d_conv]
                    if c0 in halved:
                        v = v * 0.5
                    dst[pl.ds(r0 + r, BF16_ROWS), c0:c0 + d_conv] = v.astype(bf16)
                return carry

            lax.fori_loop(0, W_STAGE_ROWS // BF16_ROWS, convert, 0)
        wa_s[...] = wa_ref[...]
        wi_s[...] = wi_ref[...]

    @pl.when(s == 0)
    def _():
        start_all(x_copy, 0)
        load_weights()
        for ref in (proj_b, u_b, gates_b, ycat_a, h_s):
            ref[...] = jnp.zeros(ref.shape, ref.dtype)

    @pl.when(s + 1 < n_steps)
    def _():
        start_all(x_copy, s + 1)

    @pl.when(s < n_steps)
    def _():
        wait_all(x_copy, s)

    @pl.when(s >= O_SLOTS + 1)
    def _():
        wait_all(out_copy, s - 1 - O_SLOTS)

    row_zero = pl.multiple_of(jnp.minimum(s, 0) * rows, rows)
    x_in0 = (jnp.minimum(s, n_steps - 1) % X_SLOTS) * blk_pos
    x_res0 = (jnp.maximum(s - 1, 0) % X_SLOTS) * blk_pos
    o_out0 = ((s + 1) % O_SLOTS) * blk_pos

    def stage_in(pos0, proj_w, proj_r, u_w, gates_w):
        per = BF16_ROWS // SUBLANES
        for q in range(n_pos // per):
            xn = [_rms_rows(xbuf[x_in0 + pos0 + p], pv(PV_LN_G))
                  for p in range(q * per, (q + 1) * per)]
            xn_s[q * BF16_ROWS:(q + 1) * BF16_ROWS, :] = jnp.concatenate(xn, axis=0).astype(bf16)
        for col, width in ((col_xl, d_lru), (col_c, d_conv), (col_x, d_conv),
                           (col_b, d_conv), (col_g, d_conv), (col_gl, d_lru)):
            for c0 in range(col, col + width, MXU_COLS):
                proj_w[pl.ds(hist, rows), c0:c0 + MXU_COLS] = jnp.dot(
                    xn_s[...], w_in_s[:, c0:c0 + MXU_COLS],
                    preferred_element_type=f32)
        for col, width in ((col_c, d_conv), (col_xl, d_lru)):
            proj_w[0:hist, col:col + width] = proj_r[rows:rows + hist, col:col + width]
        for p in range(n_pos):
            cur = slice(hist + p * SUBLANES, hist + (p + 1) * SUBLANES)
            proj_w[cur, col_c:col_c + d_conv] = (
                proj_w[cur, col_c:col_c + d_conv] * proj_w[cur, col_x:col_x + d_conv])
            u = pv(PV_LCONV_B)
            for k in range(4):
                back = slice(hist + (p - 3 + k) * SUBLANES, hist + (p - 2 + k) * SUBLANES)
                u = u + proj_w[back, col_xl:col_xl + d_lru] * pv(PV_LCONV_W + k)
            u_w[p * SUBLANES:(p + 1) * SUBLANES, :] = u
        for blk in range(rows // NORM_ROWS):
            rs = slice(blk * NORM_ROWS, (blk + 1) * NORM_ROWS)
            ub_s[rs, :] = u_w[rs, :].astype(bf16)
        for g in range(d_lru // GATE_GROUP):
            cols = slice(g * GATE_GROUP, (g + 1) * GATE_GROUP)
            ug = ub_s[:, cols]
            gates_w[:, cols] = jnp.dot(ug, wa_s[g], preferred_element_type=f32)
            gates_w[:, d_lru + g * GATE_GROUP:d_lru + (g + 1) * GATE_GROUP] = jnp.dot(
                ug, wi_s[g], preferred_element_type=f32)

    def one_position(p, h, proj_r, u_r, gates_r):
        cur = slice(hist + p * SUBLANES, hist + (p + 1) * SUBLANES)
        rows_p = slice(p * SUBLANES, (p + 1) * SUBLANES)
        conv = proj_r[cur, col_c:col_c + d_conv] * pv(PV_CONV_W + 2)
        for k in (1, 0):
            back = slice(hist + (p - 2 + k) * SUBLANES, hist + (p - 1 + k) * SUBLANES)
            conv = conv + proj_r[back, col_c:col_c + d_conv] * pv(PV_CONV_W + k)
        y = proj_r[cur, col_b:col_b + d_conv] * conv
        gate = _silu_from_half(proj_r[cur, col_g:col_g + d_conv])
        parts = []
        for hd in range(N_CONV_HEADS):
            yh = y[:, hd * LANES:(hd + 1) * LANES]
            ms = jnp.mean(yh * yh, axis=-1, keepdims=True)
            parts.append(yh * lax.rsqrt(ms + RMS_EPS))
        y_conv = (jnp.concatenate(parts, axis=-1) * pv(PV_CONV_G)) * gate

        hu = u_r[rows_p, :]
        t_r = jnp.tanh(gates_r[rows_p, 0:d_lru] + pv(PV_B_A))
        t_i = jnp.tanh(gates_r[rows_p, d_lru:2 * d_lru] + pv(PV_B_I))
        log_a = t_r * pv(PV_C_LAM) + pv(PV_C_LAM)
        a = jnp.exp(log_a)
        mult = jnp.sqrt(-jnp.tanh(log_a) * (a * a + 1.0))
        h = a * h + mult * (t_i * hu + hu)
        gl = _silu_from_half(proj_r[cur, col_gl:col_gl + d_lru])
        parts = []
        inv_n = 1.0 / (LANES // 2)
        for v in range(d_lru // LANES):
            hv = h[:, v * LANES:(v + 1) * LANES]
            sq = hv * hv
            ss_lo = jnp.sum(jnp.where(low_half, sq, 0.0), axis=-1, keepdims=True)
            ss_hi = jnp.sum(jnp.where(low_half, 0.0, sq), axis=-1, keepdims=True)
            ss = jnp.where(low_half, ss_lo, ss_hi)
            parts.append(hv * lax.rsqrt(ss * inv_n + RMS_EPS))
        y_lru = (jnp.concatenate(parts, axis=-1) * pv(PV_LRU_G)) * gl
        return y_conv, y_lru, h

    def stage_mix(proj_r, u_r, gates_r, ycat_w, h):
        per = BF16_ROWS // SUBLANES
        for q in range(n_pos // per):
            ycs, yls = [], []
            for p in range(q * per, (q + 1) * per):
                yc, yl, h = one_position(p, h, proj_r, u_r, gates_r)
                ycs.append(yc)
                yls.append(yl)
            out_rows = slice(q * BF16_ROWS, (q + 1) * BF16_ROWS)
            ycat_w[out_rows, 0:d_conv] = jnp.concatenate(ycs, axis=0).astype(bf16)
            ycat_w[out_rows, d_conv:d_conv + d_lru] = jnp.concatenate(yls, axis=0).astype(bf16)
        return h

    def stage_out(ycat_r, pos0):
        d_model = acc_s.shape[1]
        for c0 in range(0, d_model, MXU_COLS):
            acc_s[pl.ds(row_zero, rows), c0:c0 + MXU_COLS] = jnp.dot(
                ycat_r[...], w_out_s[:, c0:c0 + MXU_COLS], preferred_element_type=f32)
        for p in range(n_pos):
            hres = xbuf[x_res0 + pos0 + p] + acc_s[p * SUBLANES:(p + 1) * SUBLANES, :]
            obuf[o_out0 + pos0 + p] = _rms_rows(hres, pv(PV_FINAL_G))

    def half(pos0, set_w, set_r, ycat_w, ycat_r, h):
        proj_w, u_w, gates_w = set_w
        proj_r, u_r, gates_r = set_r
        h = stage_mix(proj_r, u_r, gates_r, ycat_w, h)
        stage_in(pos0, proj_w, proj_r, u_w, gates_w)
        stage_out(ycat_r, pos0)
        return h

    set_a = (proj_a, u_a, gates_a)
    set_b = (proj_b, u_b, gates_b)
    h = half(0, set_a, set_b, ycat_b, ycat_a, h_s[...])
    h = jnp.where(s == 0, jnp.zeros_like(h), h)
    h = half(n_pos, set_b, set_a, ycat_a, ycat_b, h)
    h_s[...] = h

    @pl.when(s >= 1)
    def _():
        start_all(out_copy, s - 1)

    @pl.when(s == n_steps)
    def _():
        for back in range(O_SLOTS, 0, -1):
            wait_all(out_copy, s - back)


def _block_diag_groups(w, group):
    heads, hd, _ = w.shape
    per = group // hd
    wg = w.reshape(heads // per, per, hd, hd)
    eye = jnp.eye(per, dtype=w.dtype)
    dense = jnp.einsum('gpde,pq->gpdqe', wg, eye)
    return dense.reshape(heads // per, group, group)


@functools.partial(jax.jit, static_argnames=("chunk_pos",))
def _forward(x, ln_g, w_in, conv_w, lru_conv_w, lru_conv_b, w_a, b_a, w_i, b_i,
             lam, conv_out_g, lru_out_g, w_out, final_g, chunk_pos=32):
    bsz, seq, d_model = x.shape
    d_conv = conv_w.shape[1]
    d_lru = lru_conv_w.shape[1]
    in_cols = w_in.shape[1]
    assert bsz == SUBLANES and in_cols == 4 * d_conv + 2 * d_lru
    assert d_conv == d_lru == d_model
    assert seq % (2 * chunk_pos) == 0 and d_lru % GATE_GROUP == 0
    assert w_in.shape[0] % W_STAGE_ROWS == 0 and w_out.shape[0] % W_STAGE_ROWS == 0
    rows = chunk_pos * SUBLANES
    assert rows >= W_STAGES_PER_BUF * W_STAGE_ROWS
    hist = (lru_conv_w.shape[0] - 1) * SUBLANES
    n_steps = seq // (2 * chunk_pos)
    assert n_steps > O_SLOTS
    blk_pos = 2 * chunk_pos
    bf16 = jnp.bfloat16
    f32 = jnp.float32

    half_c_lam = (0.5 * RG_LRU_C) * jax.nn.log_sigmoid(lam.astype(f32))
    table = jnp.stack([ln_g, final_g, conv_out_g, lru_out_g, 0.5 * b_a, 0.5 * b_i, half_c_lam,
                       0.5 * lru_conv_b, *conv_w, *(0.5 * lru_conv_w)]).astype(f32)
    assert table.shape[0] == PV_LCONV_W + lru_conv_w.shape[0] == PV_CONV_W + conv_w.shape[0] + 4
    table8 = jnp.broadcast_to(table[:, None, :], (table.shape[0], SUBLANES, d_model))
    wa_g = _block_diag_groups(w_a, GATE_GROUP).astype(bf16)
    wi_g = _block_diag_groups(w_i, GATE_GROUP).astype(bf16)

    def const(shape):
        return pl.BlockSpec(shape, lambda s: (0,) * len(shape), pipeline_mode=pl.Buffered(1))

    kern = functools.partial(_layer_kernel, rows=rows, d_conv=d_conv, d_lru=d_lru, hist=hist,
                             n_steps=n_steps)
    return pl.pallas_call(
        kern,
        grid=(n_steps + 1,),
        in_specs=[
            pl.BlockSpec(memory_space=pl.ANY),
            pl.BlockSpec(memory_space=pl.ANY),
            pl.BlockSpec(memory_space=pl.ANY),
            const(wa_g.shape),
            const(wi_g.shape),
            const(table8.shape),
        ],
        out_specs=pl.BlockSpec(memory_space=pl.ANY),
        out_shape=jax.ShapeDtypeStruct((bsz, seq, d_model), f32),
        scratch_shapes=[
            pltpu.VMEM((rows + hist, in_cols), f32),
            pltpu.VMEM((rows + hist, in_cols), f32),
            pltpu.VMEM((rows, d_lru), f32),
            pltpu.VMEM((rows, d_lru), f32),
            pltpu.VMEM((rows, 2 * d_lru), f32),
            pltpu.VMEM((rows, 2 * d_lru), f32),
            pltpu.VMEM((rows, d_conv + d_lru), bf16),
            pltpu.VMEM((rows, d_conv + d_lru), bf16),
            pltpu.VMEM((rows, d_model), bf16),
            pltpu.VMEM((rows, d_lru), bf16),
            pltpu.VMEM((SUBLANES, d_lru), f32),
            pltpu.VMEM((rows, d_model), f32),
            pltpu.VMEM((X_SLOTS * blk_pos, bsz, d_model), f32),
            pltpu.VMEM((O_SLOTS * blk_pos, bsz, d_model), f32),
            pltpu.VMEM(w_in.shape, bf16),
            pltpu.VMEM(w_out.shape, bf16),
            pltpu.VMEM(wa_g.shape, bf16),
            pltpu.VMEM(wi_g.shape, bf16),
            pltpu.SemaphoreType.DMA((X_SLOTS,)),
            pltpu.SemaphoreType.DMA((O_SLOTS,)),
            pltpu.SemaphoreType.DMA((2 * W_STAGES_PER_BUF,)),
        ],
        compiler_params=pltpu.CompilerParams(
            dimension_semantics=("arbitrary",),
            vmem_limit_bytes=VMEM_LIMIT_BYTES),
        name="hybrid_layer",
    )(x, w_in, w_out, wa_g, wi_g, table8)


def kernel(x, ln_g, w_in, conv_w, lru_conv_w, lru_conv_b, w_a, b_a, w_i, b_i, lam,
           conv_out_g, lru_out_g, w_out, final_g):
    return _forward(x, ln_g, w_in, conv_w, lru_conv_w, lru_conv_b, w_a, b_a, w_i, b_i,
                    lam, conv_out_g, lru_out_g, w_out, final_g)
```
